```python
import jax, jax.numpy as jnp
from jax import lax
import numpy as np

D_MODEL = 2048
BATCH = 8
SEQ = 4096
DEPTH = 2

GRID_W = 64
CTX_LEN = 256
HEAD_DIM = 128
NA_HEADS = 8
GQA_Q_HEADS = 8
GQA_KV_HEADS = 2
GQA_GROUP = GQA_Q_HEADS // GQA_KV_HEADS
WIN_H = 8
WIN_W = 16
Q_BLOCK = 128
ROPE_THETA = 10000.0
NA_COLS = NA_HEADS * HEAD_DIM
GQA_Q_COLS = GQA_Q_HEADS * HEAD_DIM
GQA_KV_COLS = GQA_KV_HEADS * HEAD_DIM
Q_COLS = NA_COLS + GQA_Q_COLS
KV_COLS = 2 * NA_COLS + 2 * GQA_KV_COLS
MIX_IN_COLS = Q_COLS + KV_COLS
MIX_OUT_COLS = NA_COLS + GQA_Q_COLS
CONV_WIDTH = 3
DENSE_FF = 5632
N_EXPERTS = 8
TOP_K = 2
EXPERT_FF = 7168
MOE_BLOCK = 512
NORM_EPS = 1e-6

kernel_name = "hybrid_natten_gqa_shortconv_moe_dit"

F32 = jnp.float32


def rms_norm(x, g):
    xf = x.astype(F32)
    y = xf * lax.rsqrt(jnp.mean(xf * xf, axis=-1, keepdims=True) + NORM_EPS)
    return (y * g.astype(F32)).astype(x.dtype)


def modulate(h, shift, scale):
    return h * (1 + scale) + shift


def adaln_terms(cond, w, b):
    m = jnp.dot(jax.nn.silu(cond), w) + b
    return jnp.split(m, 6, axis=-1)


def axial_rope_tables(n_tokens):
    t = jnp.arange(n_tokens)
    row = (t // GRID_W).astype(F32)
    col = (t % GRID_W).astype(F32)
    half = HEAD_DIM // 2
    freqs = ROPE_THETA ** (-jnp.arange(0, half, 2, dtype=F32) / half)
    ang = jnp.concatenate([row[:, None] * freqs, col[:, None] * freqs], axis=-1)
    return jnp.cos(ang), jnp.sin(ang)


def apply_rope(x, cos, sin):
    xf = x.astype(F32).reshape(*x.shape[:-1], HEAD_DIM // 2, 2)
    x1, x2 = xf[..., 0], xf[..., 1]
    c = cos[None, :, None, :]
    s = sin[None, :, None, :]
    out = jnp.stack([x1 * c - x2 * s, x1 * s + x2 * c], axis=-1).reshape(x.shape)
    return out.astype(x.dtype)


def grouped_attention(q, k, v):
    scale = q.shape[-1] ** -0.5
    s = jnp.einsum('bqgrd,bkgd->bgrqk', q, k, preferred_element_type=F32) * scale
    p = jax.nn.softmax(s, axis=-1).astype(v.dtype)
    return jnp.einsum('bgrqk,bkgd->bqgrd', p, v)


def gqa_block_sweep(q, k_all, v_all):
    b, s, hq, d = q.shape
    nb = s // Q_BLOCK
    qb = q.reshape(b, nb, Q_BLOCK, GQA_KV_HEADS, GQA_GROUP, d).transpose(1, 0, 2, 3, 4, 5)
    out = lax.map(lambda qi: grouped_attention(qi, k_all, v_all), qb)
    return out.transpose(1, 0, 2, 3, 4, 5).reshape(b, s, hq * d)


def neighborhood_attention(q, k, v, k_ctx, v_ctx, rpb):
    b, s, h, d = q.shape
    rows = s // GRID_W
    kh = min(WIN_H, rows)
    n_ctx = k_ctx.shape[1]
    scale = d ** -0.5
    q_rows = q.reshape(b, rows, GRID_W, h, d).transpose(1, 0, 2, 3, 4)
    k_grid = k.reshape(b, rows, GRID_W, h, d)
    v_grid = v.reshape(b, rows, GRID_W, h, d)
    cols = np.arange(GRID_W)
    col_start = np.clip(cols - WIN_W // 2, 0, GRID_W - WIN_W)
    col_valid = (cols[None, :] >= col_start[:, None]) & (cols[None, :] < col_start[:, None] + WIN_W)
    dc_idx = np.clip(cols[None, :] - cols[:, None] + WIN_W - 1, 0, 2 * WIN_W - 2)
    band = jnp.arange(kh)

    def one_row(args):
        r, q_r = args
        r0 = jnp.clip(r - kh // 2, 0, rows - kh)
        k_band = lax.dynamic_slice_in_dim(k_grid, r0, kh, axis=1)
        v_band = lax.dynamic_slice_in_dim(v_grid, r0, kh, axis=1)
        dr_idx = r0 + band - r + WIN_H - 1
        bias = rpb[:, dr_idx[:, None, None], dc_idx[None]]
        bias = bias.transpose(0, 2, 1, 3).astype(F32)
        s_loc = jnp.einsum('bqhd,bijhd->bhqij', q_r, k_band, preferred_element_type=F32) * scale + bias
        s_loc = jnp.where(col_valid[None, None, :, None, :], s_loc, -jnp.inf)
        s_loc = s_loc.reshape(b, h, GRID_W, kh * GRID_W)
        s_ctx = jnp.einsum('bqhd,blhd->bhql', q_r, k_ctx, preferred_element_type=F32) * scale
        p = jax.nn.softmax(jnp.concatenate([s_ctx, s_loc], axis=-1), axis=-1).astype(v.dtype)
        p_loc = p[..., n_ctx:].reshape(b, h, GRID_W, kh, GRID_W)
        return (jnp.einsum('bhql,blhd->bqhd', p[..., :n_ctx], v_ctx)
                + jnp.einsum('bhqij,bijhd->bqhd', p_loc, v_band))

    out = lax.map(one_row, (jnp.arange(rows), q_rows))
    return out.transpose(1, 0, 2, 3, 4).reshape(b, s, h * d)


def split_q(p, b, n):
    qa = p[..., :NA_COLS].reshape(b, n, NA_HEADS, HEAD_DIM)
    qb = p[..., NA_COLS:].reshape(b, n, GQA_Q_HEADS, HEAD_DIM)
    return qa, qb


def split_kv(p, b, n):
    ka, va, kb, vb = jnp.split(p, [NA_COLS, 2 * NA_COLS, 2 * NA_COLS + GQA_KV_COLS], axis=-1)
    return (ka.reshape(b, n, NA_HEADS, HEAD_DIM), va.reshape(b, n, NA_HEADS, HEAD_DIM),
            kb.reshape(b, n, GQA_KV_HEADS, HEAD_DIM), vb.reshape(b, n, GQA_KV_HEADS, HEAD_DIM))


def attention_mixer(h, hc, w_in, w_out, rpb, q_g, k_g, cos, sin, update_ctx):
    b, s, _ = h.shape
    n_ctx = hc.shape[1]
    proj = jnp.dot(h, w_in)
    qa, qb = split_q(proj[..., :Q_COLS], b, s)
    ka, va, kb, vb = split_kv(proj[..., Q_COLS:], b, s)
    qb = apply_rope(rms_norm(qb, q_g), cos, sin)
    kb = apply_rope(rms_norm(kb, k_g), cos, sin)
    if update_ctx:
        proj_c = jnp.dot(hc, w_in)
        kv_c = proj_c[..., Q_COLS:]
    else:
        kv_c = jnp.dot(hc, w_in[:, Q_COLS:])
    ka_c, va_c, kb_c, vb_c = split_kv(kv_c, b, n_ctx)
    kb_c = rms_norm(kb_c, k_g)
    out_a = neighborhood_attention(qa, ka, va, ka_c, va_c, rpb)
    out_b = gqa_block_sweep(qb, jnp.concatenate([kb_c, kb], axis=1), jnp.concatenate([vb_c, vb], axis=1))
    y = jnp.dot(jnp.concatenate([out_a, out_b], axis=-1), w_out)
    y_c = None
    if update_ctx:
        qa_c, qb_c = split_q(proj_c[..., :Q_COLS], b, n_ctx)
        qb_c = rms_norm(qb_c, q_g)
        oa_c = grouped_attention(qa_c[:, :, :, None, :], ka_c, va_c).reshape(b, n_ctx, NA_COLS)
        ob_c = grouped_attention(qb_c.reshape(b, n_ctx, GQA_KV_HEADS, GQA_GROUP, HEAD_DIM),
                                 kb_c, vb_c).reshape(b, n_ctx, GQA_Q_COLS)
        y_c = jnp.dot(jnp.concatenate([oa_c, ob_c], axis=-1), w_out)
    return y, y_c


def short_conv_mixer(h, w_in, w_conv, w_out):
    bg, cg, xv = jnp.split(jnp.dot(h, w_in), 3, axis=-1)
    u = cg * xv
    y = lax.conv_general_dilated(u, w_conv[:, None, :], window_strides=(1,), padding=((1, 1),),
                                 dimension_numbers=('NWC', 'WIO', 'NWC'), feature_group_count=u.shape[-1])
    return jnp.dot(bg * y, w_out)


def swiglu(h, w_gate, w_up, w_down):
    return jnp.dot(jax.nn.silu(jnp.dot(h, w_gate)) * jnp.dot(h, w_up), w_down)


def moe_swiglu(h, w_router, w_gate, w_up, w_down):
    d = h.shape[-1]
    hf = h.reshape(-1, d)
    n = hf.shape[0]
    logits = jnp.dot(hf, w_router, preferred_element_type=F32)
    top_logit, top_e = lax.top_k(logits, TOP_K)
    gates = jax.nn.softmax(top_logit, axis=-1)
    n_assign = n * TOP_K
    flat_e = top_e.reshape(-1)
    flat_tok = jnp.arange(n_assign, dtype=jnp.int32) // TOP_K
    order = jnp.argsort(flat_e)
    e_sorted = flat_e[order]
    tok_sorted = flat_tok[order]
    g_sorted = gates.reshape(-1)[order]
    counts = jnp.bincount(flat_e, length=N_EXPERTS)
    padded = (counts + MOE_BLOCK - 1) // MOE_BLOCK * MOE_BLOCK
    pad_end = jnp.cumsum(padded)
    pad_start = pad_end - padded
    grp_start = jnp.cumsum(counts) - counts
    dest = pad_start[e_sorted] + jnp.arange(n_assign) - grp_start[e_sorted]
    n_blocks = -(-n_assign // MOE_BLOCK) + N_EXPERTS
    buf_tok = jnp.zeros((n_blocks * MOE_BLOCK,), jnp.int32).at[dest].set(tok_sorted)
    block_e = jnp.minimum(jnp.searchsorted(pad_end, jnp.arange(n_blocks) * MOE_BLOCK, side='right'),
                          N_EXPERTS - 1)

    def expert_block(args):
        e, toks = args
        xb = hf[toks]
        return swiglu(xb, w_gate[e], w_up[e], w_down[e])

    y_buf = lax.map(expert_block, (block_e, buf_tok.reshape(n_blocks, MOE_BLOCK))).reshape(-1, d)
    y = jnp.zeros_like(hf).at[tok_sorted].add(y_buf[dest] * g_sorted[:, None].astype(hf.dtype))
    return y.reshape(h.shape)


def setup_inputs(seed: int = 0) -> dict:
    key = jax.random.key(seed)
    ks = jax.random.split(key, 26)
    D = D_MODEL
    n_even = (DEPTH + 1) // 2
    n_odd = DEPTH // 2

    def nrm(k, shape, fan_in, gain=1.0):
        return jax.random.normal(k, shape, F32) * (gain * fan_in ** -0.5)

    def gain(k, shape):
        return 1.0 + 0.05 * jax.random.normal(k, shape, F32)

    return {
        "x": jax.random.normal(ks[0], (BATCH, SEQ, D), F32),
        "c": jax.random.normal(ks[1], (BATCH, D), F32),
        "ctx": jax.random.normal(ks[2], (BATCH, CTX_LEN, D), F32),
        "c_ctx": jax.random.normal(ks[3], (D,), F32),
        "ada_w": nrm(ks[4], (DEPTH, D, 6 * D), D, 0.5),
        "ada_b": 0.02 * jax.random.normal(ks[5], (DEPTH, 6 * D), F32),
        "norm1_g": gain(ks[6], (DEPTH, D)),
        "norm2_g": gain(ks[7], (DEPTH, D)),
        "attn_w_in": nrm(ks[8], (n_even, D, MIX_IN_COLS), D),
        "attn_w_out": nrm(ks[9], (n_even, MIX_OUT_COLS, D), MIX_OUT_COLS),
        "na_rpb": 0.1 * jax.random.normal(ks[10], (n_even, NA_HEADS, 2 * WIN_H - 1, 2 * WIN_W - 1), F32),
        "gqa_q_norm_g": gain(ks[11], (n_even, HEAD_DIM)),
        "gqa_k_norm_g": gain(ks[12], (n_even, HEAD_DIM)),
        "ffn_w_gate": nrm(ks[13], (n_even, D, DENSE_FF), D),
        "ffn_w_up": nrm(ks[14], (n_even, D, DENSE_FF), D),
        "ffn_w_down": nrm(ks[15], (n_even, DENSE_FF, D), DENSE_FF),
        "conv_w_in": nrm(ks[16], (n_odd, D, 3 * D), D),
        "conv_w": nrm(ks[17], (n_odd, CONV_WIDTH, D), CONV_WIDTH),
        "conv_w_out": nrm(ks[18], (n_odd, D, D), D),
        "moe_w_router": nrm(ks[19], (n_odd, D, N_EXPERTS), D),
        "moe_w_gate": nrm(ks[20], (n_odd, N_EXPERTS, D, EXPERT_FF), D),
        "moe_w_up": nrm(ks[21], (n_odd, N_EXPERTS, D, EXPERT_FF), D),
        "moe_w_down": nrm(ks[22], (n_odd, N_EXPERTS, EXPERT_FF, D), EXPERT_FF),
        "final_norm_g": gain(ks[23], (D,)),
    }


def reference(x, c, ctx, c_ctx, ada_w, ada_b, norm1_g, norm2_g, attn_w_in, attn_w_out, na_rpb,
              gqa_q_norm_g, gqa_k_norm_g, ffn_w_gate, ffn_w_up, ffn_w_down, conv_w_in, conv_w, conv_w_out,
              moe_w_router, moe_w_gate, moe_w_up, moe_w_down, final_norm_g):
    cos, sin = axial_rope_tables(x.shape[1])
    last_attn = ((DEPTH - 1) // 2) * 2
    for i in range(DEPTH):
        j = i // 2
        even = i % 2 == 0
        update_ctx = i < last_attn
        sh1, sc1, g1, sh2, sc2, g2 = [t[:, None, :] for t in adaln_terms(c, ada_w[i], ada_b[i])]
        h = modulate(rms_norm(x, norm1_g[i]), sh1, sc1)
        if even or update_ctx:
            csh1, csc1, cg1, csh2, csc2, cg2 = adaln_terms(c_ctx, ada_w[i], ada_b[i])
            hc = modulate(rms_norm(ctx, norm1_g[i]), csh1, csc1)
        if even:
            mix, mix_c = attention_mixer(h, hc, attn_w_in[j], attn_w_out[j], na_rpb[j],
                                         gqa_q_norm_g[j], gqa_k_norm_g[j], cos, sin, update_ctx)

            def ffn(t):
                return swiglu(t, ffn_w_gate[j], ffn_w_up[j], ffn_w_down[j])
        else:
            mix = short_conv_mixer(h, conv_w_in[j], conv_w[j], conv_w_out[j])
            mix_c = short_conv_mixer(hc, conv_w_in[j], conv_w[j], conv_w_out[j]) if update_ctx else None

            def ffn(t):
                return moe_swiglu(t, moe_w_router[j], moe_w_gate[j], moe_w_up[j], moe_w_down[j])
        x = x + g1 * mix
        x = x + g2 * ffn(modulate(rms_norm(x, norm2_g[i]), sh2, sc2))
        if update_ctx:
            ctx = ctx + cg1 * mix_c
            ctx = ctx + cg2 * ffn(modulate(rms_norm(ctx, norm2_g[i]), csh2, csc2))
    return rms_norm(x, final_norm_g)
```

```python
import functools

import numpy as np
import jax
import jax.numpy as jnp
from jax import lax
from jax.experimental import pallas as pl
from jax.experimental.pallas import tpu as pltpu

F32 = jnp.float32
BF16 = jnp.bfloat16
I32 = jnp.int32
U32 = jnp.uint32

GRID_W = 64
HEAD_DIM = 128
NA_HEADS = 8
GQA_Q_HEADS = 8
GQA_KV_HEADS = 2
GQA_GROUP = GQA_Q_HEADS // GQA_KV_HEADS
WIN_H = 8
WIN_W = 16
ROPE_THETA = 10000.0
N_EXPERTS = 8
NORM_EPS = 1e-6
NA_COLS = NA_HEADS * HEAD_DIM
GQA_Q_COLS = GQA_Q_HEADS * HEAD_DIM
GQA_KV_COLS = GQA_KV_HEADS * HEAD_DIM
Q_COLS = NA_COLS + GQA_Q_COLS
KV_COLS = 2 * NA_COLS + 2 * GQA_KV_COLS

LANES = 128
SUBLANES = 8
V7X_VMEM_LIMIT = 56 * 1024 * 1024
MASK_VALUE = -1e30
NT_DIMS = (((1,), (1,)), ((), ()))


def _tile(dim, pref):
    return pref if dim % pref == 0 else dim


def _params(sem, vmem=V7X_VMEM_LIMIT):
    return pltpu.CompilerParams(dimension_semantics=sem, vmem_limit_bytes=vmem)


def _norm_mod(x, g, shift, scale):
    y = x * lax.rsqrt(jnp.mean(x * x, axis=-1, keepdims=True) + NORM_EPS)
    return (y * g) * (1.0 + scale) + shift


def _adaln_kernel(cond_ref, w_ref, b_ref, o_ref):
    a = jax.nn.silu(cond_ref[...])
    o_ref[0] = jnp.dot(a, w_ref[0], preferred_element_type=F32,
                       precision=lax.Precision.HIGHEST) + b_ref[0]


def adaln_terms(cond, ada_w, ada_b):
    L, D, N = ada_w.shape
    R = cond.shape[0]
    tn = _tile(N, 1024)
    return pl.pallas_call(
        _adaln_kernel,
        grid=(L, N // tn),
        in_specs=[
            pl.BlockSpec((R, D), lambda l, j: (0, 0)),
            pl.BlockSpec((1, D, tn), lambda l, j: (l, 0, j)),
            pl.BlockSpec((1, 1, tn), lambda l, j: (l, 0, j)),
        ],
        out_specs=pl.BlockSpec((1, R, tn), lambda l, j: (l, 0, j)),
        out_shape=jax.ShapeDtypeStruct((L, R, N), F32),
        compiler_params=_params(("parallel", "parallel")),
        name="adaln",
    )(cond, ada_w, ada_b.reshape(L, 1, N))


def _nm_matmul_kernel(x_ref, g_ref, sh_ref, sc_ref, w_ref, o_ref, h_ref):
    @pl.when(pl.program_id(2) == 0)
    def _():
        h_ref[...] = _norm_mod(x_ref[0], g_ref[...], sh_ref[0], sc_ref[0]).astype(BF16)

    o_ref[0] = jnp.dot(h_ref[...], w_ref[...], preferred_element_type=F32).astype(o_ref.dtype)


def norm_mod_matmul(x, g, shift, scale, w, *, col_off=0, n_cols=None, tm=1024, tn=512):
    B, S, D = x.shape
    n_cols = w.shape[1] - col_off if n_cols is None else n_cols
    tm = _tile(S, tm)
    tn = _tile(n_cols, tn)
    assert col_off % tn == 0
    joff = col_off // tn
    return pl.pallas_call(
        _nm_matmul_kernel,
        grid=(B, S // tm, n_cols // tn),
        in_specs=[
            pl.BlockSpec((1, tm, D), lambda b, i, j: (b, i, 0)),
            pl.BlockSpec((1, D), lambda b, i, j: (0, 0)),
            pl.BlockSpec((1, 1, D), lambda b, i, j: (b, 0, 0)),
            pl.BlockSpec((1, 1, D), lambda b, i, j: (b, 0, 0)),
            pl.BlockSpec((D, tn), lambda b, i, j: (0, j + joff)),
        ],
        out_specs=pl.BlockSpec((1, tm, tn), lambda b, i, j: (b, i, j)),
        out_shape=jax.ShapeDtypeStruct((B, S, n_cols), BF16),
        scratch_shapes=[pltpu.VMEM((tm, D), BF16)],
        compiler_params=_params(("parallel", "parallel", "arbitrary")),
        name="norm_mod_matmul",
    )(x, g.reshape(1, D), shift, scale, w)


def _head_norm_kernel(x_ref, g_ref, cos_ref, sin_ref, o_ref, *, n_heads, rope, out_scale):
    g = g_ref[...]
    for h in range(n_heads):
        sl = slice(h * HEAD_DIM, (h + 1) * HEAD_DIM)
        xh = x_ref[0, :, sl].astype(F32)
        y = xh * lax.rsqrt(jnp.mean(xh * xh, axis=-1, keepdims=True) + NORM_EPS) * g
        if rope:
            lane = lax.broadcasted_iota(I32, y.shape, 1)
            swap = jnp.where((lane & 1) == 0, pltpu.roll(y, HEAD_DIM - 1, 1), pltpu.roll(y, 1, 1))
            y = y * cos_ref[...] + swap * sin_ref[...]
        if out_scale != 1.0:
            y = y * out_scale
        o_ref[0, :, sl] = y.astype(o_ref.dtype)


def head_norm(src, col_block, n_heads, g, cos, sin, *, rope, out_scale=1.0, tm=512):
    B, S, _ = src.shape
    w = n_heads * HEAD_DIM
    tm = _tile(S, tm)
    kern = functools.partial(_head_norm_kernel, n_heads=n_heads, rope=rope, out_scale=out_scale)
    return pl.pallas_call(
        kern,
        grid=(B, S // tm),
        in_specs=[
            pl.BlockSpec((1, tm, w), lambda b, i: (b, i, col_block)),
            pl.BlockSpec((1, HEAD_DIM), lambda b, i: (0, 0)),
            pl.BlockSpec((tm, HEAD_DIM), lambda b, i: (i, 0)),
            pl.BlockSpec((tm, HEAD_DIM), lambda b, i: (i, 0)),
        ],
        out_specs=pl.BlockSpec((1, tm, w), lambda b, i: (b, i, 0)),
        out_shape=jax.ShapeDtypeStruct((B, S, w), BF16),
        compiler_params=_params(("parallel", "parallel")),
        name="head_norm",
    )(src, g.reshape(1, HEAD_DIM), cos, sin)


def _na_kernel(q_ref, k_ref, v_ref, kc_ref, vc_ref, bias_ref, o_ref, *, rows, kh, scale):
    kc = kc_ref[0]
    vc = vc_ref[0]

    def body(r, carry):
        r0 = jnp.clip(r - kh // 2, 0, rows - kh)
        q = q_ref[0, pl.ds(pl.multiple_of(r * GRID_W, GRID_W), GRID_W), :]
        band = pl.ds(pl.multiple_of(r0 * GRID_W, GRID_W), kh * GRID_W)
        kb = k_ref[0, band, :]
        vb = v_ref[0, band, :]
        s_loc = lax.dot_general(q, kb, NT_DIMS, preferred_element_type=F32) * scale + bias_ref[0, r - r0]
        s_ctx = lax.dot_general(q, kc, NT_DIMS, preferred_element_type=F32) * scale
        m = jnp.maximum(jnp.max(s_loc, axis=-1, keepdims=True), jnp.max(s_ctx, axis=-1, keepdims=True))
        p_loc = jnp.exp(s_loc - m)
        p_ctx = jnp.exp(s_ctx - m)
        l = jnp.sum(p_loc, axis=-1, keepdims=True) + jnp.sum(p_ctx, axis=-1, keepdims=True)
        o = (jnp.dot(p_ctx.astype(BF16), vc, preferred_element_type=F32)
             + jnp.dot(p_loc.astype(BF16), vb, preferred_element_type=F32))
        o_ref[0, pl.ds(pl.multiple_of(r * GRID_W, GRID_W), GRID_W), :] = (o / l).astype(o_ref.dtype)
        return carry

    lax.fori_loop(0, rows, body, 0)


def na_bias_table(rpb, kh):
    cols = np.arange(GRID_W)
    col_start = np.clip(cols - WIN_W // 2, 0, GRID_W - WIN_W)
    col_valid = (cols[None, :] >= col_start[:, None]) & (cols[None, :] < col_start[:, None] + WIN_W)
    dc_idx = np.clip(cols[None, :] - cols[:, None] + WIN_W - 1, 0, 2 * WIN_W - 2)
    dr_idx = np.arange(kh)[None, :] - np.arange(kh)[:, None] + WIN_H - 1
    tab = rpb.astype(F32)[:, dr_idx[:, :, None, None], dc_idx[None, None]]
    tab = jnp.where(col_valid[None, None, None], tab, MASK_VALUE)
    return tab.transpose(0, 1, 3, 2, 4).reshape(rpb.shape[0], kh, GRID_W, kh * GRID_W)


def neighborhood_attention(proj, kvc, bias):
    B, S, _ = proj.shape
    L = kvc.shape[1]
    rows = S // GRID_W
    kh = bias.shape[1]
    qb, kb, vb = 0, Q_COLS // HEAD_DIM, (Q_COLS + NA_COLS) // HEAD_DIM
    kern = functools.partial(_na_kernel, rows=rows, kh=kh, scale=HEAD_DIM ** -0.5)
    return pl.pallas_call(
        kern,
        grid=(B, NA_HEADS),
        in_specs=[
            pl.BlockSpec((1, S, HEAD_DIM), lambda b, h: (b, 0, qb + h)),
            pl.BlockSpec((1, S, HEAD_DIM), lambda b, h: (b, 0, kb + h)),
            pl.BlockSpec((1, S, HEAD_DIM), lambda b, h: (b, 0, vb + h)),
            pl.BlockSpec((1, L, HEAD_DIM), lambda b, h: (b, 0, h)),
            pl.BlockSpec((1, L, HEAD_DIM), lambda b, h: (b, 0, NA_HEADS + h)),
            pl.BlockSpec((1, kh, GRID_W, kh * GRID_W), lambda b, h: (h, 0, 0, 0)),
        ],
        out_specs=pl.BlockSpec((1, S, HEAD_DIM), lambda b, h: (b, 0, h)),
        out_shape=jax.ShapeDtypeStruct((B, S, NA_COLS), BF16),
        compiler_params=_params(("parallel", "parallel")),
        name="na_attention",
    )(proj, proj, proj, kvc, kvc, bias)


def _gqa_kernel(q_ref, k_ref, v_ref, kc_ref, vc_ref, o_ref):
    k = k_ref[0]
    v = v_ref[0]
    kc = kc_ref[0]
    vc = vc_ref[0]
    for r in range(GQA_GROUP):
        sl = slice(r * HEAD_DIM, (r + 1) * HEAD_DIM)
        q = q_ref[0, :, sl]
        s_ctx = lax.dot_general(q, kc, NT_DIMS, preferred_element_type=F32)
        s_lat = lax.dot_general(q, k, NT_DIMS, preferred_element_type=F32)
        m = jnp.maximum(jnp.max(s_ctx, axis=-1, keepdims=True), jnp.max(s_lat, axis=-1, keepdims=True))
        p_ctx = jnp.exp(s_ctx - m)
        p_lat = jnp.exp(s_lat - m)
        l = jnp.sum(p_ctx, axis=-1, keepdims=True) + jnp.sum(p_lat, axis=-1, keepdims=True)
        o = (jnp.dot(p_ctx.astype(BF16), vc, preferred_element_type=F32)
             + jnp.dot(p_lat.astype(BF16), v, preferred_element_type=F32))
        o_ref[0, :, sl] = (o / l).astype(o_ref.dtype)


def gqa_attention(qn, kn, proj, kcn, kvc, *, tq=256):
    B, S, _ = qn.shape
    L = kcn.shape[1]
    tq = _tile(S, tq)
    gw = GQA_GROUP * HEAD_DIM
    vblk = (Q_COLS + 2 * NA_COLS + GQA_KV_COLS) // HEAD_DIM
    vcblk = (2 * NA_COLS + GQA_KV_COLS) // HEAD_DIM
    return pl.pallas_call(
        _gqa_kernel,
        grid=(B, GQA_KV_HEADS, S // tq),
        in_specs=[
            pl.BlockSpec((1, tq, gw), lambda b, g, i: (b, i, g)),
            pl.BlockSpec((1, S, HEAD_DIM), lambda b, g, i: (b, 0, g)),
            pl.BlockSpec((1, S, HEAD_DIM), lambda b, g, i: (b, 0, vblk + g)),
            pl.BlockSpec((1, L, HEAD_DIM), lambda b, g, i: (b, 0, g)),
            pl.BlockSpec((1, L, HEAD_DIM), lambda b, g, i: (b, 0, vcblk + g)),
        ],
        out_specs=pl.BlockSpec((1, tq, gw), lambda b, g, i: (b, i, g)),
        out_shape=jax.ShapeDtypeStruct((B, S, GQA_Q_COLS), BF16),
        compiler_params=_params(("parallel", "parallel", "parallel")),
        name="gqa_attention",
    )(qn, kn, proj, kcn, kvc)


def _mm_res_kernel(*refs, k_sizes):
    n = len(k_sizes)
    w_ref, x_ref, gate_ref, o_ref = refs[n:]
    acc = None
    off = 0
    for a_ref, ks in zip(refs[:n], k_sizes):
        part = jnp.dot(a_ref[0], w_ref[off:off + ks, :], preferred_element_type=F32)
        acc = part if acc is None else acc + part
        off += ks
    o_ref[0] = x_ref[0] + gate_ref[0] * acc


def matmul_residual(lhs_list, w, x, gate, *, tm=512, tn=2048):
    B, S, N = x.shape
    K = w.shape[0]
    k_sizes = tuple(a.shape[-1] for a in lhs_list)
    assert sum(k_sizes) == K
    tm = _tile(S, tm)
    tn = _tile(N, tn)
    lhs_specs = [pl.BlockSpec((1, tm, ks), lambda b, i, j: (b, i, 0)) for ks in k_sizes]
    return pl.pallas_call(
        functools.partial(_mm_res_kernel, k_sizes=k_sizes),
        grid=(B, S // tm, N // tn),
        in_specs=lhs_specs + [
            pl.BlockSpec((K, tn), lambda b, i, j: (0, j)),
            pl.BlockSpec((1, tm, tn), lambda b, i, j: (b, i, j)),
            pl.BlockSpec((1, 1, tn), lambda b, i, j: (b, 0, j)),
        ],
        out_specs=pl.BlockSpec((1, tm, tn), lambda b, i, j: (b, i, j)),
        out_shape=jax.ShapeDtypeStruct((B, S, N), F32),
        compiler_params=_params(("parallel", "parallel", "arbitrary")),
        name="matmul_residual",
    )(*lhs_list, w, x, gate)


DOWN_CHUNK = 512


def _accumulate_down(acc_ref, a, wd_ref):
    n = acc_ref.shape[1]
    step = DOWN_CHUNK if n % DOWN_CHUNK == 0 else n
    for n0 in range(0, n, step):
        acc_ref[:, n0:n0 + step] += jnp.dot(a, wd_ref[:, n0:n0 + step], preferred_element_type=F32)


def _ffn_kernel(x_ref, g_ref, sh_ref, sc_ref, gate_ref, wg_ref, wu_ref, wd_ref, o_ref, h_ref, acc_ref):
    j = pl.program_id(2)

    @pl.when(j == 0)
    def _():
        h_ref[...] = _norm_mod(x_ref[0], g_ref[...], sh_ref[0], sc_ref[0]).astype(BF16)
        acc_ref[...] = jnp.zeros_like(acc_ref)

    h = h_ref[...]
    a = (jax.nn.silu(jnp.dot(h, wg_ref[...], preferred_element_type=F32))
         * jnp.dot(h, wu_ref[...], preferred_element_type=F32)).astype(BF16)
    _accumulate_down(acc_ref, a, wd_ref)

    @pl.when(j == pl.num_programs(2) - 1)
    def _():
        o_ref[0] = x_ref[0] + gate_ref[0] * acc_ref[...]


def dense_ffn(x, g, shift, scale, gate, wg, wu, wd, *, tm=512, tf=512):
    B, S, D = x.shape
    FF = wg.shape[1]
    tm = _tile(S, tm)
    tf = _tile(FF, tf)
    vec = pl.BlockSpec((1, 1, D), lambda b, i, j: (b, 0, 0))
    return pl.pallas_call(
        _ffn_kernel,
        grid=(B, S // tm, FF // tf),
        in_specs=[
            pl.BlockSpec((1, tm, D), lambda b, i, j: (b, i, 0)),
            pl.BlockSpec((1, D), lambda b, i, j: (0, 0)),
            vec, vec, vec,
            pl.BlockSpec((D, tf), lambda b, i, j: (0, j)),
            pl.BlockSpec((D, tf), lambda b, i, j: (0, j)),
            pl.BlockSpec((tf, D), lambda b, i, j: (j, 0)),
        ],
        out_specs=pl.BlockSpec((1, tm, D), lambda b, i, j: (b, i, 0)),
        out_shape=jax.ShapeDtypeStruct((B, S, D), F32),
        scratch_shapes=[pltpu.VMEM((tm, D), BF16), pltpu.VMEM((tm, D), F32)],
        compiler_params=_params(("parallel", "parallel", "arbitrary")),
        name="dense_ffn",
    )(x, g.reshape(1, D), shift, scale, gate, wg, wu, wd)


def _conv_kernel(x_ref, xp_ref, xn_ref, g_ref, sh_ref, sc_ref, wb_ref, wc_ref, wx_ref, cw_ref,
                 o_ref, h_ref, hh_ref):
    i = pl.program_id(1)
    tm = h_ref.shape[0]

    @pl.when(pl.program_id(2) == 0)
    def _():
        g, sh, sc = g_ref[...], sh_ref[0], sc_ref[0]
        h_ref[...] = _norm_mod(x_ref[0], g, sh, sc).astype(BF16)
        halo = jnp.concatenate([xp_ref[0], xn_ref[0]], axis=0)
        hh_ref[...] = _norm_mod(halo, g, sh, sc).astype(BF16)

    h = h_ref[...]
    hh = hh_ref[...]
    wc = wc_ref[...]
    wx = wx_ref[...]
    bg = jnp.dot(h, wb_ref[...], preferred_element_type=F32)
    u = jnp.dot(h, wc, preferred_element_type=F32) * jnp.dot(h, wx, preferred_element_type=F32)
    uh = jnp.dot(hh, wc, preferred_element_type=F32) * jnp.dot(hh, wx, preferred_element_type=F32)
    prev = jnp.where(i > 0, uh[SUBLANES - 1:SUBLANES, :], 0.0)
    nxt = jnp.where(i < pl.num_programs(1) - 1, uh[SUBLANES:SUBLANES + 1, :], 0.0)
    row = lax.broadcasted_iota(I32, u.shape, 0)
    up = jnp.where(row == 0, prev, pltpu.roll(u, 1, 0))
    dn = jnp.where(row == tm - 1, nxt, pltpu.roll(u, tm - 1, 0))
    cw = cw_ref[...]
    y = cw[0:1, :] * up + cw[1:2, :] * u + cw[2:3, :] * dn
    o_ref[0] = (bg * y).astype(o_ref.dtype)


def conv_mixer_inner(x, g, shift, scale, w_in, conv_w, *, tm=512, tc=512):
    B, S, D = x.shape
    tm = _tile(S, tm)
    tc = _tile(D, tc)
    nc = D // tc
    hb = tm // SUBLANES
    last_hb = S // SUBLANES - 1
    vec = pl.BlockSpec((1, 1, D), lambda b, i, j: (b, 0, 0))
    return pl.pallas_call(
        _conv_kernel,
        grid=(B, S // tm, nc),
        in_specs=[
            pl.BlockSpec((1, tm, D), lambda b, i, j: (b, i, 0)),
            pl.BlockSpec((1, SUBLANES, D), lambda b, i, j: (b, jnp.maximum(i * hb - 1, 0), 0)),
            pl.BlockSpec((1, SUBLANES, D), lambda b, i, j: (b, jnp.minimum((i + 1) * hb, last_hb), 0)),
            pl.BlockSpec((1, D), lambda b, i, j: (0, 0)),
            vec, vec,
            pl.BlockSpec((D, tc), lambda b, i, j: (0, j)),
            pl.BlockSpec((D, tc), lambda b, i, j: (0, nc + j)),
            pl.BlockSpec((D, tc), lambda b, i, j: (0, 2 * nc + j)),
            pl.BlockSpec((3, tc), lambda b, i, j: (0, j)),
        ],
        out_specs=pl.BlockSpec((1, tm, tc), lambda b, i, j: (b, i, j)),
        out_shape=jax.ShapeDtypeStruct((B, S, D), BF16),
        scratch_shapes=[pltpu.VMEM((tm, D), BF16), pltpu.VMEM((2 * SUBLANES, D), BF16)],
        compiler_params=_params(("parallel", "parallel", "arbitrary")),
        name="conv_mixer",
    )(x, x, x, g.reshape(1, D), shift, scale, w_in, w_in, w_in, conv_w)


def _router_kernel(x_ref, g_ref, sh_ref, sc_ref, wr_ref, hp_ref, idx_ref, gcol_ref, cnt_ref, run_ref):
    tm, D = x_ref.shape[1], x_ref.shape[2]
    d2 = D // 2

    @pl.when((pl.program_id(0) == 0) & (pl.program_id(1) == 0))
    def _():
        run_ref[...] = jnp.zeros_like(run_ref)

    h = _norm_mod(x_ref[0], g_ref[...], sh_ref[0], sc_ref[0])

    lo = pltpu.bitcast(h[:, :d2].astype(BF16).astype(F32), U32)
    hi = pltpu.bitcast(h[:, d2:].astype(BF16).astype(F32), U32)
    hp_ref[0] = (hi & jnp.uint32(0xFFFF0000)) | (lo >> 16)

    logits = jnp.dot(h, wr_ref[...], preferred_element_type=F32, precision=lax.Precision.HIGHEST)
    lane = lax.broadcasted_iota(I32, logits.shape, 1).astype(F32)
    neg = jnp.float32(-jnp.inf)
    l1 = jnp.where(lane < N_EXPERTS, logits, neg)
    m1 = jnp.max(l1, axis=-1, keepdims=True)
    i1 = jnp.min(jnp.where(l1 == m1, lane, float(LANES)), axis=-1, keepdims=True)
    l2 = jnp.where(lane == i1, neg, l1)
    m2 = jnp.max(l2, axis=-1, keepdims=True)
    i2 = jnp.min(jnp.where(l2 == m2, lane, float(LANES)), axis=-1, keepdims=True)
    e = jnp.exp(m2 - m1)
    den = 1.0 + e
    gcol_ref[...] = jnp.where(lane == 0, 1.0 / den, jnp.where(lane == 1, e / den, 0.0))

    chosen = jnp.where((lane == i1) | (lane == i2), 1.0, 0.0)
    chosen_t = chosen.T
    rr = lax.broadcasted_iota(I32, (tm, tm), 0)
    cc = lax.broadcasted_iota(I32, (tm, tm), 1)
    before = jnp.where(rr < cc, 1.0, 0.0).astype(BF16)
    cum = jnp.dot(chosen_t.astype(BF16), before, preferred_element_type=F32) + run_ref[...]
    sel = jnp.where(lane == 0, i1, jnp.where(lane == 1, i2, 0.0)).T
    i1r = sel[0:1, :]
    i2r = sel[1:2, :]
    sub = lax.broadcasted_iota(I32, cum.shape, 0).astype(F32)
    p1r = jnp.sum(jnp.where(sub == i1r, cum, 0.0), axis=0, keepdims=True)
    p2r = jnp.sum(jnp.where(sub == i2r, cum, 0.0), axis=0, keepdims=True)
    idx_ref[0] = jnp.concatenate([i1r, i2r, p1r, p2r], axis=0).astype(I32)

    run_ref[...] += jnp.sum(chosen_t, axis=1, keepdims=True)
    cnt_ref[...] = jnp.broadcast_to(run_ref[...], cnt_ref.shape)


def moe_router(x, g, shift, scale, w_router, *, tm=512):
    B, S, D = x.shape
    E = w_router.shape[1]
    tm = _tile(S, tm)
    nS = S // tm
    wr = jnp.zeros((D, LANES), F32).at[:, :E].set(w_router.astype(F32))
    vec = pl.BlockSpec((1, 1, D), lambda b, i: (b, 0, 0))
    hp, idx, gcol, cnt = pl.pallas_call(
        _router_kernel,
        grid=(B, nS),
        in_specs=[
            pl.BlockSpec((1, tm, D), lambda b, i: (b, i, 0)),
            pl.BlockSpec((1, D), lambda b, i: (0, 0)),
            vec, vec,
            pl.BlockSpec((D, LANES), lambda b, i: (0, 0)),
        ],
        out_specs=[
            pl.BlockSpec((1, tm, D // 2), lambda b, i: (b, i, 0)),
            pl.BlockSpec((1, 4, tm), lambda b, i: (b * nS + i, 0, 0)),
            pl.BlockSpec((tm, LANES), lambda b, i: (b * nS + i, 0)),
            pl.BlockSpec((LANES, LANES), lambda b, i: (0, 0)),
        ],
        out_shape=[
            jax.ShapeDtypeStruct((B, S, D // 2), U32),
            jax.ShapeDtypeStruct((B * nS, 4, tm), I32),
            jax.ShapeDtypeStruct((B * S, LANES), F32),
            jax.ShapeDtypeStruct((LANES, LANES), F32),
        ],
        scratch_shapes=[pltpu.VMEM((LANES, 1), F32)],
        compiler_params=_params(("arbitrary", "arbitrary")),
        name="moe_router",
    )(x, g.reshape(1, D), shift, scale, wr)
    return hp.reshape(B * S, D // 2), idx, gcol, cnt


def _row_copy(src, dst, sem):
    return pltpu.make_async_copy(src, dst, sem)


def _dispatch_kernel(pstart_ref, idx_ref, hp_ref, xs_in_ref, xs_ref, sem):
    del xs_in_ref
    tm = hp_ref.shape[0]

    def start(t, c):
        for k in range(2):
            d = pstart_ref[idx_ref[0, k, t]] + idx_ref[0, 2 + k, t]
            _row_copy(hp_ref.at[pl.ds(t, 1)], xs_ref.at[pl.ds(d, 1)], sem).start()
        return c

    lax.fori_loop(0, tm, start, 0)

    def wait(t, c):
        _row_copy(hp_ref.at[pl.ds(0, 1)], xs_ref.at[pl.ds(0, 1)], sem).wait()
        return c

    lax.fori_loop(0, 2 * tm, wait, 0)


def moe_dispatch(pad_start, idx, hp, n_slots):
    N, d2 = hp.shape
    nT, _, tm = idx.shape
    xs0 = jnp.zeros((n_slots, d2), U32)
    grid_spec = pltpu.PrefetchScalarGridSpec(
        num_scalar_prefetch=1,
        grid=(nT,),
        in_specs=[
            pl.BlockSpec((1, 4, tm), lambda t, ps: (t, 0, 0), memory_space=pltpu.SMEM),
            pl.BlockSpec((tm, d2), lambda t, ps: (t, 0)),
            pl.BlockSpec(memory_space=pl.ANY),
        ],
        out_specs=pl.BlockSpec(memory_space=pl.ANY),
        scratch_shapes=[pltpu.SemaphoreType.DMA(())],
    )
    return pl.pallas_call(
        _dispatch_kernel,
        grid_spec=grid_spec,
        out_shape=jax.ShapeDtypeStruct((n_slots, d2), U32),
        input_output_aliases={3: 0},
        compiler_params=_params(("arbitrary",)),
        name="moe_dispatch",
    )(pad_start, idx, hp, xs0)


def _moe_ffn_kernel(be_ref, nv_ref, xs_ref, wg_ref, wu_ref, wd_ref, y_ref, xb_ref):
    b = pl.program_id(0)
    j = pl.program_id(1)
    d2 = xs_ref.shape[1]

    @pl.when((b >= nv_ref[0]) & (j == 0))
    def _():
        y_ref[...] = jnp.zeros_like(y_ref)

    @pl.when(b < nv_ref[0])
    def _():
        @pl.when(j == 0)
        def _():
            w = xs_ref[...]
            xb_ref[:, :d2] = pltpu.bitcast(w << 16, F32).astype(BF16)
            xb_ref[:, d2:] = pltpu.bitcast(w & jnp.uint32(0xFFFF0000), F32).astype(BF16)

            y_ref[...] = jnp.zeros_like(y_ref)

        x = xb_ref[...]
        a = (jax.nn.silu(jnp.dot(x, wg_ref[0], preferred_element_type=F32))
             * jnp.dot(x, wu_ref[0], preferred_element_type=F32)).astype(BF16)
        _accumulate_down(y_ref, a, wd_ref.at[0])


def moe_expert_ffn(block_e, n_valid, xs, wg, wu, wd, *, tm, tf=512):
    n_slots, d2 = xs.shape
    D = 2 * d2
    FF = wg.shape[2]
    tf = _tile(FF, tf)
    nj = FF // tf
    nb = n_slots // tm

    def row_map(b, j, be, nv):
        return (jnp.minimum(b, nv[0] - 1), 0)

    def col_of(b, j, nv):
        return jnp.where(b < nv[0], j, nj - 1)

    grid_spec = pltpu.PrefetchScalarGridSpec(
        num_scalar_prefetch=2,
        grid=(nb, nj),
        in_specs=[
            pl.BlockSpec((tm, d2), row_map),
            pl.BlockSpec((1, D, tf), lambda b, j, be, nv: (be[b], 0, col_of(b, j, nv))),
            pl.BlockSpec((1, D, tf), lambda b, j, be, nv: (be[b], 0, col_of(b, j, nv))),
            pl.BlockSpec((1, tf, D), lambda b, j, be, nv: (be[b], col_of(b, j, nv), 0)),
        ],
        out_specs=pl.BlockSpec((tm, D), lambda b, j, be, nv: (b, 0)),
        scratch_shapes=[pltpu.VMEM((tm, D), BF16)],
    )
    return pl.pallas_call(
        _moe_ffn_kernel,
        grid_spec=grid_spec,
        out_shape=jax.ShapeDtypeStruct((n_slots, D), F32),
        compiler_params=_params(("arbitrary", "arbitrary")),
        name="moe_expert_ffn",
    )(block_e, n_valid, xs, wg, wu, wd)


def _combine_kernel(pstart_ref, idx_ref, x_ref, gcol_ref, gate_ref, fg_ref, y_ref, o_ref, ya_ref, yb_ref, sem):
    tm = x_ref.shape[1]
    bufs = (ya_ref, yb_ref)

    def start(t, c):
        for k in range(2):
            d = pstart_ref[idx_ref[0, k, t]] + idx_ref[0, 2 + k, t]
            _row_copy(y_ref.at[pl.ds(d, 1)], bufs[k].at[pl.ds(t, 1)], sem).start()
        return c

    lax.fori_loop(0, tm, start, 0)

    def wait(t, c):
        _row_copy(y_ref.at[pl.ds(0, 1)], ya_ref.at[pl.ds(0, 1)], sem).wait()
        return c

    lax.fori_loop(0, 2 * tm, wait, 0)

    moe = gcol_ref[:, 0:1] * ya_ref[...] + gcol_ref[:, 1:2] * yb_ref[...]
    x = x_ref[0] + gate_ref[0] * moe
    y = x * lax.rsqrt(jnp.mean(x * x, axis=-1, keepdims=True) + NORM_EPS)
    o_ref[0] = y * fg_ref[...]


def moe_combine(pad_start, idx, x, gcol, gate, final_g, y_buf):
    B, S, D = x.shape
    nT, _, tm = idx.shape
    nS = S // tm
    grid_spec = pltpu.PrefetchScalarGridSpec(
        num_scalar_prefetch=1,
        grid=(B, nS),
        in_specs=[
            pl.BlockSpec((1, 4, tm), lambda b, i, ps: (b * nS + i, 0, 0), memory_space=pltpu.SMEM),
            pl.BlockSpec((1, tm, D), lambda b, i, ps: (b, i, 0)),
            pl.BlockSpec((tm, LANES), lambda b, i, ps: (b * nS + i, 0)),
            pl.BlockSpec((1, 1, D), lambda b, i, ps: (b, 0, 0)),
            pl.BlockSpec((1, D), lambda b, i, ps: (0, 0)),
            pl.BlockSpec(memory_space=pl.ANY),
        ],
        out_specs=pl.BlockSpec((1, tm, D), lambda b, i, ps: (b, i, 0)),
        scratch_shapes=[pltpu.VMEM((tm, D), F32), pltpu.VMEM((tm, D), F32), pltpu.SemaphoreType.DMA(())],
    )
    return pl.pallas_call(
        _combine_kernel,
        grid_spec=grid_spec,
        out_shape=jax.ShapeDtypeStruct((B, S, D), F32),
        compiler_params=_params(("arbitrary", "arbitrary")),
        name="moe_combine",
    )(pad_start, idx, x, gcol, gate, final_g.reshape(1, D), y_buf)


def moe_layout(cnt, n_tokens, blk):
    counts = cnt[:N_EXPERTS, 0].astype(I32)
    padded = (counts + blk - 1) // blk * blk
    pad_end = jnp.cumsum(padded)
    pad_start = (pad_end - padded).astype(I32)
    n_blocks = (2 * n_tokens) // blk + N_EXPERTS
    n_valid = (pad_end[-1] // blk).astype(I32)
    starts = jnp.minimum(jnp.arange(n_blocks, dtype=I32), n_valid - 1) * blk
    block_e = jnp.minimum(jnp.searchsorted(pad_end, starts, side="right"), N_EXPERTS - 1).astype(I32)
    return pad_start, block_e, n_valid.reshape(1), n_blocks * blk


def rope_tables(n_tokens):
    t = jnp.arange(n_tokens)
    row = (t // GRID_W).astype(F32)
    col = (t % GRID_W).astype(F32)
    half = HEAD_DIM // 2
    freqs = ROPE_THETA ** (-jnp.arange(0, half, 2, dtype=F32) / half)
    ang = jnp.concatenate([row[:, None] * freqs, col[:, None] * freqs], axis=-1)
    cos = jnp.repeat(jnp.cos(ang), 2, axis=-1)
    sin = jnp.repeat(jnp.sin(ang), 2, axis=-1) * jnp.tile(jnp.array([-1.0, 1.0], F32), half)
    return cos, sin


def kernel(x, c, ctx, c_ctx, ada_w, ada_b, norm1_g, norm2_g, attn_w_in, attn_w_out, na_rpb, gqa_q_norm_g,
           gqa_k_norm_g, ffn_w_gate, ffn_w_up, ffn_w_down, conv_w_in, conv_w, conv_w_out, moe_w_router,
           moe_w_gate, moe_w_up, moe_w_down, final_norm_g):
    B, S, D = x.shape
    assert ada_w.shape[0] == 2 and S % GRID_W == 0 and S // GRID_W >= WIN_H
    kh = WIN_H

    n_cond = -(-(B + 1) // SUBLANES) * SUBLANES
    cond = jnp.zeros((n_cond, D), F32).at[:B].set(c).at[B].set(c_ctx)
    mod = adaln_terms(cond, ada_w, ada_b)

    def terms(layer, rows):
        m = mod[layer, rows].reshape(-1, 6, D)
        return [m[:, k][:, None, :] for k in range(6)]

    sh1, sc1, g1, sh2, sc2, g2 = terms(0, slice(0, B))
    csh1, csc1 = [jnp.broadcast_to(t, (B, 1, D)) for t in terms(0, slice(B, B + 1))[:2]]

    w_in = attn_w_in[0].astype(BF16)
    proj = norm_mod_matmul(x, norm1_g[0], sh1, sc1, w_in)
    kvc = norm_mod_matmul(ctx, norm1_g[0], csh1, csc1, w_in, col_off=Q_COLS, tm=256)
    cos, sin = rope_tables(S)
    qn = head_norm(proj, NA_COLS // GQA_Q_COLS, GQA_Q_HEADS, gqa_q_norm_g[0], cos, sin, rope=True,
                   out_scale=HEAD_DIM ** -0.5)
    kn = head_norm(proj, (Q_COLS + 2 * NA_COLS) // GQA_KV_COLS, GQA_KV_HEADS, gqa_k_norm_g[0], cos, sin, rope=True)
    L = ctx.shape[1]
    kcn = head_norm(kvc, 2 * NA_COLS // GQA_KV_COLS, GQA_KV_HEADS, gqa_k_norm_g[0], cos[:L], sin[:L], rope=False,
                    tm=L)
    out_a = neighborhood_attention(proj, kvc, na_bias_table(na_rpb[0], kh))
    out_b = gqa_attention(qn, kn, proj, kcn, kvc)
    x = matmul_residual([out_a, out_b], attn_w_out[0].astype(BF16), x, g1)

    x = dense_ffn(x, norm2_g[0], sh2, sc2, g2, ffn_w_gate[0].astype(BF16), ffn_w_up[0].astype(BF16),
                  ffn_w_down[0].astype(BF16))

    sh1, sc1, g1, sh2, sc2, g2 = terms(1, slice(0, B))
    z = conv_mixer_inner(x, norm1_g[1], sh1, sc1, conv_w_in[0].astype(BF16), conv_w[0])
    x = matmul_residual([z], conv_w_out[0].astype(BF16), x, g1)

    moe_blk = 1024 if (2 * B * S) % 1024 == 0 else 2 * B * S // N_EXPERTS
    hp, idx, gcol, cnt = moe_router(x, norm2_g[1], sh2, sc2, moe_w_router[0])
    pad_start, block_e, n_valid, n_slots = moe_layout(cnt, B * S, moe_blk)
    xs = moe_dispatch(pad_start, idx, hp, n_slots)
    y_buf = moe_expert_ffn(block_e, n_valid, xs, moe_w_gate[0].astype(BF16), moe_w_up[0].astype(BF16),
                           moe_w_down[0].astype(BF16), tm=moe_blk)
    return moe_combine(pad_start, idx, x, gcol, g2, final_norm_g, y_buf)
```

```python
import functools

import numpy as np
import jax
import jax.numpy as jnp
from jax import lax
from jax.experimental import pallas as pl
from jax.experimental.pallas import tpu as pltpu

F32 = jnp.float32
BF16 = jnp.bfloat16
I32 = jnp.int32
U32 = jnp.uint32

GRID_W = 64
HEAD_DIM = 128
NA_HEADS = 8
GQA_Q_HEADS = 8
GQA_KV_HEADS = 2
GQA_GROUP = GQA_Q_HEADS // GQA_KV_HEADS
WIN_H = 8
WIN_W = 16
ROPE_THETA = 10000.0
N_EXPERTS = 8
NORM_EPS = 1e-6
NA_COLS = NA_HEADS * HEAD_DIM
GQA_Q_COLS = GQA_Q_HEADS * HEAD_DIM
GQA_KV_COLS = GQA_KV_HEADS * HEAD_DIM
Q_COLS = NA_COLS + GQA_Q_COLS
KV_COLS = 2 * NA_COLS + 2 * GQA_KV_COLS

LANES = 128
SUBLANES = 8
V7X_VMEM_LIMIT = 56 * 1024 * 1024
MASK_VALUE = -1e30
NT_DIMS = (((1,), (1,)), ((), ()))
LOG2_E = 1.4426950408889634


def _tile(dim, pref):
    return pref if dim % pref == 0 else dim


def _params(sem, vmem=V7X_VMEM_LIMIT):
    return pltpu.CompilerParams(dimension_semantics=sem, vmem_limit_bytes=vmem)


def _norm_mod(x, g, shift, scale):
    y = x * lax.rsqrt(jnp.mean(x * x, axis=-1, keepdims=True) + NORM_EPS)
    return (y * g) * (1.0 + scale) + shift


def _adaln_kernel(cond_ref, w_ref, b_ref, o_ref):
    a = jax.nn.silu(cond_ref[...])
    o_ref[0] = jnp.dot(a, w_ref[0], preferred_element_type=F32,
                       precision=lax.Precision.HIGHEST) + b_ref[0]


def adaln_terms(cond, ada_w, ada_b):
    L, D, N = ada_w.shape
    R = cond.shape[0]
    tn = _tile(N, 1024)
    return pl.pallas_call(
        _adaln_kernel,
        grid=(L, N // tn),
        in_specs=[
            pl.BlockSpec((R, D), lambda l, j: (0, 0)),
            pl.BlockSpec((1, D, tn), lambda l, j: (l, 0, j)),
            pl.BlockSpec((1, 1, tn), lambda l, j: (l, 0, j)),
        ],
        out_specs=pl.BlockSpec((1, R, tn), lambda l, j: (l, 0, j)),
        out_shape=jax.ShapeDtypeStruct((L, R, N), F32),
        compiler_params=_params(("parallel", "parallel")),
        name="adaln",
    )(cond, ada_w, ada_b.reshape(L, 1, N))


def _nm_matmul_kernel(x_ref, g_ref, sh_ref, sc_ref, w_ref, o_ref, h_ref):
    @pl.when(pl.program_id(2) == 0)
    def _():
        h_ref[...] = _norm_mod(x_ref[0], g_ref[...], sh_ref[0], sc_ref[0]).astype(BF16)

    o_ref[0] = jnp.dot(h_ref[...], w_ref[...], preferred_element_type=F32).astype(o_ref.dtype)


def norm_mod_matmul(x, g, shift, scale, w, *, col_off=0, n_cols=None, tm=1024, tn=512):
    B, S, D = x.shape
    n_cols = w.shape[1] - col_off if n_cols is None else n_cols
    tm = _tile(S, tm)
    tn = _tile(n_cols, tn)
    assert col_off % tn == 0
    joff = col_off // tn
    return pl.pallas_call(
        _nm_matmul_kernel,
        grid=(B, S // tm, n_cols // tn),
        in_specs=[
            pl.BlockSpec((1, tm, D), lambda b, i, j: (b, i, 0)),
            pl.BlockSpec((1, D), lambda b, i, j: (0, 0)),
            pl.BlockSpec((1, 1, D), lambda b, i, j: (b, 0, 0)),
            pl.BlockSpec((1, 1, D), lambda b, i, j: (b, 0, 0)),
            pl.BlockSpec((D, tn), lambda b, i, j: (0, j + joff)),
        ],
        out_specs=pl.BlockSpec((1, tm, tn), lambda b, i, j: (b, i, j)),
        out_shape=jax.ShapeDtypeStruct((B, S, n_cols), BF16),
        scratch_shapes=[pltpu.VMEM((tm, D), BF16)],
        compiler_params=_params(("parallel", "parallel", "arbitrary")),
        name="norm_mod_matmul",
    )(x, g.reshape(1, D), shift, scale, w)


def _head_norm_kernel(x_ref, g_ref, cos_ref, sin_ref, o_ref, *, n_heads, rope, out_scale):
    g = g_ref[...]
    for h in range(n_heads):
        sl = slice(h * HEAD_DIM, (h + 1) * HEAD_DIM)
        xh = x_ref[0, :, sl].astype(F32)
        y = xh * lax.rsqrt(jnp.mean(xh * xh, axis=-1, keepdims=True) + NORM_EPS) * g
        if rope:
            lane = lax.broadcasted_iota(I32, y.shape, 1)
            swap = jnp.where((lane & 1) == 0, pltpu.roll(y, HEAD_DIM - 1, 1), pltpu.roll(y, 1, 1))
            y = y * cos_ref[...] + swap * sin_ref[...]
        if out_scale != 1.0:
            y = y * out_scale
        o_ref[0, :, sl] = y.astype(o_ref.dtype)


def head_norm(src, col_block, n_heads, g, cos, sin, *, rope, out_scale=1.0, tm=512):
    B, S, _ = src.shape
    w = n_heads * HEAD_DIM
    tm = _tile(S, tm)
    kern = functools.partial(_head_norm_kernel, n_heads=n_heads, rope=rope, out_scale=out_scale)
    return pl.pallas_call(
        kern,
        grid=(B, S // tm),
        in_specs=[
            pl.BlockSpec((1, tm, w), lambda b, i: (b, i, col_block)),
            pl.BlockSpec((1, HEAD_DIM), lambda b, i: (0, 0)),
            pl.BlockSpec((tm, HEAD_DIM), lambda b, i: (i, 0)),
            pl.BlockSpec((tm, HEAD_DIM), lambda b, i: (i, 0)),
        ],
        out_specs=pl.BlockSpec((1, tm, w), lambda b, i: (b, i, 0)),
        out_shape=jax.ShapeDtypeStruct((B, S, w), BF16),
        compiler_params=_params(("parallel", "parallel")),
        name="head_norm",
    )(src, g.reshape(1, HEAD_DIM), cos, sin)


NA_GROUP = 4
NA_GROUP_UNROLL = 2
NA_VARIANTS = 3


def _na_kernel(q_ref, k_ref, v_ref, kc_ref, vc_ref, bias_ref, o_ref, *, rows, kh, scale):
    kc = kc_ref[0]
    vc = vc_ref[0]
    n_groups = rows // NA_GROUP
    win = NA_GROUP + kh
    gq = NA_GROUP * GRID_W

    def body(g, carry):
        w0 = jnp.clip(g * NA_GROUP - kh // 2, 0, rows - win)
        variant = jnp.where(g == 0, 0, jnp.where(g == n_groups - 1, 2, 1))
        qs = pl.ds(pl.multiple_of(g * gq, gq), gq)
        band = pl.ds(pl.multiple_of(w0 * GRID_W, GRID_W), win * GRID_W)
        q = q_ref[0, qs, :]
        kb = k_ref[0, band, :]
        vb = v_ref[0, band, :]
        s_loc = lax.dot_general(q, kb, NT_DIMS, preferred_element_type=F32) * scale + bias_ref[0, variant]
        s_ctx = lax.dot_general(q, kc, NT_DIMS, preferred_element_type=F32) * scale
        m = jnp.maximum(jnp.max(s_loc, axis=-1, keepdims=True), jnp.max(s_ctx, axis=-1, keepdims=True))
        p_loc = jnp.exp(s_loc - m)
        p_ctx = jnp.exp(s_ctx - m)
        l = jnp.sum(p_loc, axis=-1, keepdims=True) + jnp.sum(p_ctx, axis=-1, keepdims=True)
        o = (jnp.dot(p_ctx.astype(BF16), vc, preferred_element_type=F32)
             + jnp.dot(p_loc.astype(BF16), vb, preferred_element_type=F32))
        o_ref[0, qs, :] = (o / l).astype(o_ref.dtype)
        return carry

    lax.fori_loop(0, n_groups, body, 0, unroll=NA_GROUP_UNROLL)


def _bias_table_kernel(rpb_ref, row_sel_ref, col_sel_ref, valid_ref, o_ref):
    rows = jnp.dot(row_sel_ref[...], rpb_ref[0], preferred_element_type=F32, precision=lax.Precision.HIGHEST)
    tab = jnp.dot(rows, col_sel_ref[...], preferred_element_type=F32, precision=lax.Precision.HIGHEST)
    o_ref[0] = jnp.where(valid_ref[...] > 0.0, tab, MASK_VALUE)


def na_bias_table(rpb, kh):
    H, n_dr, n_dc = rpb.shape
    win = NA_GROUP + kh
    cols = np.arange(GRID_W)
    col_start = np.clip(cols - WIN_W // 2, 0, GRID_W - WIN_W)
    col_valid = (cols[None, :] >= col_start[:, None]) & (cols[None, :] < col_start[:, None] + WIN_W)
    dc_idx = np.clip(cols[None, :] - cols[:, None] + WIN_W - 1, 0, 2 * WIN_W - 2)
    ql = np.arange(NA_GROUP)[:, None]
    wr = np.arange(win)[None, :]
    band0 = [np.zeros_like(ql), ql, np.full_like(ql, win - kh)]
    dr = [wr - ql + WIN_H - 1, wr - ql + WIN_H - 1 - kh // 2, wr - ql + WIN_H - 1 - kh]
    pr = -(-(n_dr + 1) // SUBLANES) * SUBLANES
    pc = -(-n_dc // SUBLANES) * SUBLANES
    mask_row = pr - 1
    n_rows = NA_VARIANTS * NA_GROUP * win
    row_pick = np.concatenate([np.where((wr >= b0) & (wr < b0 + kh), d, mask_row).reshape(-1)
                               for b0, d in zip(band0, dr)])
    assert row_pick.min() >= 0 and (row_pick[row_pick != mask_row] < n_dr).all()
    row_sel = np.zeros((n_rows, pr), np.float32)
    row_sel[np.arange(n_rows), row_pick] = 1.0
    col_sel = np.zeros((pc, GRID_W * GRID_W), np.float32)
    col_sel[dc_idx.reshape(-1), np.arange(GRID_W * GRID_W)] = 1.0
    valid = col_valid.reshape(1, -1).astype(np.float32)
    rpb_p = jnp.zeros((H, pr, pc), F32).at[:, :n_dr, :n_dc].set(rpb.astype(F32)).at[:, mask_row, :].set(MASK_VALUE)
    tab = pl.pallas_call(
        _bias_table_kernel,
        grid=(H,),
        in_specs=[
            pl.BlockSpec((1, pr, pc), lambda h: (h, 0, 0)),
            pl.BlockSpec((n_rows, pr), lambda h: (0, 0)),
            pl.BlockSpec((pc, GRID_W * GRID_W), lambda h: (0, 0)),
            pl.BlockSpec((1, GRID_W * GRID_W), lambda h: (0, 0)),
        ],
        out_specs=pl.BlockSpec((1, n_rows, GRID_W * GRID_W), lambda h: (h, 0, 0)),
        out_shape=jax.ShapeDtypeStruct((H, n_rows, GRID_W * GRID_W), F32),
        compiler_params=_params(("parallel",)),
        name="na_bias_table",
    )(rpb_p, jnp.asarray(row_sel), jnp.asarray(col_sel), jnp.asarray(valid))
    tab = tab.reshape(H, NA_VARIANTS, NA_GROUP, win, GRID_W, GRID_W)
    return tab.transpose(0, 1, 2, 4, 3, 5).reshape(H, NA_VARIANTS, NA_GROUP * GRID_W, win * GRID_W)


def neighborhood_attention(proj, kvc, bias, kh):
    B, S, _ = proj.shape
    L = kvc.shape[1]
    rows = S // GRID_W
    assert rows % NA_GROUP == 0 and rows // NA_GROUP >= NA_VARIANTS
    qb, kb, vb = 0, Q_COLS // HEAD_DIM, (Q_COLS + NA_COLS) // HEAD_DIM
    kern = functools.partial(_na_kernel, rows=rows, kh=kh, scale=HEAD_DIM ** -0.5)
    return pl.pallas_call(
        kern,
        grid=(B, NA_HEADS),
        in_specs=[
            pl.BlockSpec((1, S, HEAD_DIM), lambda b, h: (b, 0, qb + h)),
            pl.BlockSpec((1, S, HEAD_DIM), lambda b, h: (b, 0, kb + h)),
            pl.BlockSpec((1, S, HEAD_DIM), lambda b, h: (b, 0, vb + h)),
            pl.BlockSpec((1, L, HEAD_DIM), lambda b, h: (b, 0, h)),
            pl.BlockSpec((1, L, HEAD_DIM), lambda b, h: (b, 0, NA_HEADS + h)),
            pl.BlockSpec((1,) + bias.shape[1:], lambda b, h: (h, 0, 0, 0)),
        ],
        out_specs=pl.BlockSpec((1, S, HEAD_DIM), lambda b, h: (b, 0, h)),
        out_shape=jax.ShapeDtypeStruct((B, S, NA_COLS), BF16),
        compiler_params=_params(("parallel", "parallel")),
        name="na_attention",
    )(proj, proj, proj, kvc, kvc, bias)


def _gqa_kernel(q_ref, k_ref, v_ref, kc_ref, vc_ref, o_ref):
    k = k_ref[0]
    v = v_ref[0]
    kc = kc_ref[0]
    vc = vc_ref[0]
    for r in range(GQA_GROUP):
        sl = slice(r * HEAD_DIM, (r + 1) * HEAD_DIM)
        q = q_ref[0, :, sl]
        s_ctx = lax.dot_general(q, kc, NT_DIMS, preferred_element_type=F32)
        s_lat = lax.dot_general(q, k, NT_DIMS, preferred_element_type=F32)
        m = jnp.maximum(jnp.max(s_ctx, axis=-1, keepdims=True), jnp.max(s_lat, axis=-1, keepdims=True))
        p_ctx = jnp.exp2(s_ctx - m)
        p_lat = jnp.exp2(s_lat - m)
        l = jnp.sum(p_ctx, axis=-1, keepdims=True) + jnp.sum(p_lat, axis=-1, keepdims=True)
        o = (jnp.dot(p_ctx.astype(BF16), vc, preferred_element_type=F32)
             + jnp.dot(p_lat.astype(BF16), v, preferred_element_type=F32))
        o_ref[0, :, sl] = (o / l).astype(o_ref.dtype)


def gqa_attention(qn, kn, proj, kcn, kvc, *, tq=256):
    B, S, _ = qn.shape
    L = kcn.shape[1]
    tq = _tile(S, tq)
    gw = GQA_GROUP * HEAD_DIM
    vblk = (Q_COLS + 2 * NA_COLS + GQA_KV_COLS) // HEAD_DIM
    vcblk = (2 * NA_COLS + GQA_KV_COLS) // HEAD_DIM
    return pl.pallas_call(
        _gqa_kernel,
        grid=(B, GQA_KV_HEADS, S // tq),
        in_specs=[
            pl.BlockSpec((1, tq, gw), lambda b, g, i: (b, i, g)),
            pl.BlockSpec((1, S, HEAD_DIM), lambda b, g, i: (b, 0, g)),
            pl.BlockSpec((1, S, HEAD_DIM), lambda b, g, i: (b, 0, vblk + g)),
            pl.BlockSpec((1, L, HEAD_DIM), lambda b, g, i: (b, 0, g)),
            pl.BlockSpec((1, L, HEAD_DIM), lambda b, g, i: (b, 0, vcblk + g)),
        ],
        out_specs=pl.BlockSpec((1, tq, gw), lambda b, g, i: (b, i, g)),
        out_shape=jax.ShapeDtypeStruct((B, S, GQA_Q_COLS), BF16),
        compiler_params=_params(("parallel", "parallel", "parallel")),
        name="gqa_attention",
    )(qn, kn, proj, kcn, kvc)


def _mm_res_kernel(*refs, k_sizes):
    n = len(k_sizes)
    w_ref, x_ref, gate_ref, o_ref = refs[n:]
    acc = None
    off = 0
    for a_ref, ks in zip(refs[:n], k_sizes):
        part = jnp.dot(a_ref[0], w_ref[off:off + ks, :], preferred_element_type=F32)
        acc = part if acc is None else acc + part
        off += ks
    o_ref[0] = x_ref[0] + gate_ref[0] * acc


def matmul_residual(lhs_list, w, x, gate, *, tm=512, tn=2048):
    B, S, N = x.shape
    K = w.shape[0]
    k_sizes = tuple(a.shape[-1] for a in lhs_list)
    assert sum(k_sizes) == K
    tm = _tile(S, tm)
    tn = _tile(N, tn)
    lhs_specs = [pl.BlockSpec((1, tm, ks), lambda b, i, j: (b, i, 0)) for ks in k_sizes]
    return pl.pallas_call(
        functools.partial(_mm_res_kernel, k_sizes=k_sizes),
        grid=(B, S // tm, N // tn),
        in_specs=lhs_specs + [
            pl.BlockSpec((K, tn), lambda b, i, j: (0, j)),
            pl.BlockSpec((1, tm, tn), lambda b, i, j: (b, i, j)),
            pl.BlockSpec((1, 1, tn), lambda b, i, j: (b, 0, j)),
        ],
        out_specs=pl.BlockSpec((1, tm, tn), lambda b, i, j: (b, i, j)),
        out_shape=jax.ShapeDtypeStruct((B, S, N), F32),
        compiler_params=_params(("parallel", "parallel", "arbitrary")),
        name="matmul_residual",
    )(*lhs_list, w, x, gate)


DOWN_CHUNK = 512


def _accumulate_down(acc_ref, a, wd_ref):
    n = acc_ref.shape[1]
    step = DOWN_CHUNK if n % DOWN_CHUNK == 0 else n
    for n0 in range(0, n, step):
        acc_ref[:, n0:n0 + step] += jnp.dot(a, wd_ref[:, n0:n0 + step], preferred_element_type=F32)


def _ffn_kernel(x_ref, g_ref, sh_ref, sc_ref, gate_ref, wg_ref, wu_ref, wd_ref, o_ref, h_ref, acc_ref):
    j = pl.program_id(2)

    @pl.when(j == 0)
    def _():
        h_ref[...] = _norm_mod(x_ref[0], g_ref[...], sh_ref[0], sc_ref[0]).astype(BF16)
        acc_ref[...] = jnp.zeros_like(acc_ref)

    h = h_ref[...]
    a = (jax.nn.silu(jnp.dot(h, wg_ref[...], preferred_element_type=F32))
         * jnp.dot(h, wu_ref[...], preferred_element_type=F32)).astype(BF16)
    _accumulate_down(acc_ref, a, wd_ref)

    @pl.when(j == pl.num_programs(2) - 1)
    def _():
        o_ref[0] = x_ref[0] + gate_ref[0] * acc_ref[...]


def dense_ffn(x, g, shift, scale, gate, wg, wu, wd, *, tm=512, tf=512):
    B, S, D = x.shape
    FF = wg.shape[1]
    tm = _tile(S, tm)
    tf = _tile(FF, tf)
    vec = pl.BlockSpec((1, 1, D), lambda b, i, j: (b, 0, 0))
    return pl.pallas_call(
        _ffn_kernel,
        grid=(B, S // tm, FF // tf),
        in_specs=[
            pl.BlockSpec((1, tm, D), lambda b, i, j: (b, i, 0)),
            pl.BlockSpec((1, D), lambda b, i, j: (0, 0)),
            vec, vec, vec,
            pl.BlockSpec((D, tf), lambda b, i, j: (0, j)),
            pl.BlockSpec((D, tf), lambda b, i, j: (0, j)),
            pl.BlockSpec((tf, D), lambda b, i, j: (j, 0)),
        ],
        out_specs=pl.BlockSpec((1, tm, D), lambda b, i, j: (b, i, 0)),
        out_shape=jax.ShapeDtypeStruct((B, S, D), F32),
        scratch_shapes=[pltpu.VMEM((tm, D), BF16), pltpu.VMEM((tm, D), F32)],
        compiler_params=_params(("parallel", "parallel", "arbitrary")),
        name="dense_ffn",
    )(x, g.reshape(1, D), shift, scale, gate, wg, wu, wd)


def _conv_kernel(x_ref, xp_ref, xn_ref, g_ref, sh_ref, sc_ref, wb_ref, wc_ref, wx_ref, cw_ref,
                 o_ref, h_ref, hh_ref):
    i = pl.program_id(1)
    tm = h_ref.shape[0]

    @pl.when(pl.program_id(2) == 0)
    def _():
        g, sh, sc = g_ref[...], sh_ref[0], sc_ref[0]
        h_ref[...] = _norm_mod(x_ref[0], g, sh, sc).astype(BF16)
        halo = jnp.concatenate([xp_ref[0], xn_ref[0]], axis=0)
        hh_ref[...] = _norm_mod(halo, g, sh, sc).astype(BF16)

    h = h_ref[...]
    hh = hh_ref[...]
    wc = wc_ref[...]
    wx = wx_ref[...]
    bg = jnp.dot(h, wb_ref[...], preferred_element_type=F32)
    u = jnp.dot(h, wc, preferred_element_type=F32) * jnp.dot(h, wx, preferred_element_type=F32)
    uh = jnp.dot(hh, wc, preferred_element_type=F32) * jnp.dot(hh, wx, preferred_element_type=F32)
    prev = jnp.where(i > 0, uh[SUBLANES - 1:SUBLANES, :], 0.0)
    nxt = jnp.where(i < pl.num_programs(1) - 1, uh[SUBLANES:SUBLANES + 1, :], 0.0)
    row = lax.broadcasted_iota(I32, u.shape, 0)
    up = jnp.where(row == 0, prev, pltpu.roll(u, 1, 0))
    dn = jnp.where(row == tm - 1, nxt, pltpu.roll(u, tm - 1, 0))
    cw = cw_ref[...]
    y = cw[0:1, :] * up + cw[1:2, :] * u + cw[2:3, :] * dn
    o_ref[0] = (bg * y).astype(o_ref.dtype)


def conv_mixer_inner(x, g, shift, scale, w_in, conv_w, *, tm=512, tc=512):
    B, S, D = x.shape
    tm = _tile(S, tm)
    tc = _tile(D, tc)
    nc = D // tc
    hb = tm // SUBLANES
    last_hb = S // SUBLANES - 1
    vec = pl.BlockSpec((1, 1, D), lambda b, i, j: (b, 0, 0))
    return pl.pallas_call(
        _conv_kernel,
        grid=(B, S // tm, nc),
        in_specs=[
            pl.BlockSpec((1, tm, D), lambda b, i, j: (b, i, 0)),
            pl.BlockSpec((1, SUBLANES, D), lambda b, i, j: (b, jnp.maximum(i * hb - 1, 0), 0)),
            pl.BlockSpec((1, SUBLANES, D), lambda b, i, j: (b, jnp.minimum((i + 1) * hb, last_hb), 0)),
            pl.BlockSpec((1, D), lambda b, i, j: (0, 0)),
            vec, vec,
            pl.BlockSpec((D, tc), lambda b, i, j: (0, j)),
            pl.BlockSpec((D, tc), lambda b, i, j: (0, nc + j)),
            pl.BlockSpec((D, tc), lambda b, i, j: (0, 2 * nc + j)),
            pl.BlockSpec((3, tc), lambda b, i, j: (0, j)),
        ],
        out_specs=pl.BlockSpec((1, tm, tc), lambda b, i, j: (b, i, j)),
        out_shape=jax.ShapeDtypeStruct((B, S, D), BF16),
        scratch_shapes=[pltpu.VMEM((tm, D), BF16), pltpu.VMEM((2 * SUBLANES, D), BF16)],
        compiler_params=_params(("parallel", "parallel", "arbitrary")),
        name="conv_mixer",
    )(x, x, x, g.reshape(1, D), shift, scale, w_in, w_in, w_in, conv_w)


def _router_kernel(x_ref, g_ref, sh_ref, sc_ref, wr_ref, hp_ref, idx_ref, gcol_ref, cnt_ref, run_ref):
    tm, D = x_ref.shape[1], x_ref.shape[2]
    d2 = D // 2

    @pl.when((pl.program_id(0) == 0) & (pl.program_id(1) == 0))
    def _():
        run_ref[...] = jnp.zeros_like(run_ref)

    h = _norm_mod(x_ref[0], g_ref[...], sh_ref[0], sc_ref[0])

    lo = pltpu.bitcast(h[:, :d2].astype(BF16).astype(F32), U32)
    hi = pltpu.bitcast(h[:, d2:].astype(BF16).astype(F32), U32)
    hp_ref[0] = (hi & jnp.uint32(0xFFFF0000)) | (lo >> 16)

    logits = jnp.dot(h, wr_ref[...], preferred_element_type=F32, precision=lax.Precision.HIGHEST)
    lane = lax.broadcasted_iota(I32, logits.shape, 1).astype(F32)
    neg = jnp.float32(-jnp.inf)
    l1 = jnp.where(lane < N_EXPERTS, logits, neg)
    m1 = jnp.max(l1, axis=-1, keepdims=True)
    i1 = jnp.min(jnp.where(l1 == m1, lane, float(LANES)), axis=-1, keepdims=True)
    l2 = jnp.where(lane == i1, neg, l1)
    m2 = jnp.max(l2, axis=-1, keepdims=True)
    i2 = jnp.min(jnp.where(l2 == m2, lane, float(LANES)), axis=-1, keepdims=True)
    e = jnp.exp(m2 - m1)
    den = 1.0 + e
    gcol_ref[...] = jnp.where(lane == 0, 1.0 / den, jnp.where(lane == 1, e / den, 0.0))

    chosen = jnp.where((lane == i1) | (lane == i2), 1.0, 0.0)
    chosen_t = chosen.T
    rr = lax.broadcasted_iota(I32, (tm, tm), 0)
    cc = lax.broadcasted_iota(I32, (tm, tm), 1)
    before = jnp.where(rr < cc, 1.0, 0.0).astype(BF16)
    cum = jnp.dot(chosen_t.astype(BF16), before, preferred_element_type=F32) + run_ref[...]
    sel = jnp.where(lane == 0, i1, jnp.where(lane == 1, i2, 0.0)).T
    i1r = sel[0:1, :]
    i2r = sel[1:2, :]
    sub = lax.broadcasted_iota(I32, cum.shape, 0).astype(F32)
    p1r = jnp.sum(jnp.where(sub == i1r, cum, 0.0), axis=0, keepdims=True)
    p2r = jnp.sum(jnp.where(sub == i2r, cum, 0.0), axis=0, keepdims=True)
    idx_ref[0] = jnp.concatenate([i1r, i2r, p1r, p2r], axis=0).astype(I32)

    run_ref[...] += jnp.sum(chosen_t, axis=1, keepdims=True)
    cnt_ref[...] = jnp.broadcast_to(run_ref[...], cnt_ref.shape)


def moe_router(x, g, shift, scale, w_router, *, tm=512):
    B, S, D = x.shape
    E = w_router.shape[1]
    tm = _tile(S, tm)
    nS = S // tm
    wr = jnp.zeros((D, LANES), F32).at[:, :E].set(w_router.astype(F32))
    vec = pl.BlockSpec((1, 1, D), lambda b, i: (b, 0, 0))
    hp, idx, gcol, cnt = pl.pallas_call(
        _router_kernel,
        grid=(B, nS),
        in_specs=[
            pl.BlockSpec((1, tm, D), lambda b, i: (b, i, 0)),
            pl.BlockSpec((1, D), lambda b, i: (0, 0)),
            vec, vec,
            pl.BlockSpec((D, LANES), lambda b, i: (0, 0)),
        ],
        out_specs=[
            pl.BlockSpec((1, tm, D // 2), lambda b, i: (b, i, 0)),
            pl.BlockSpec((1, 4, tm), lambda b, i: (b * nS + i, 0, 0)),
            pl.BlockSpec((tm, LANES), lambda b, i: (b * nS + i, 0)),
            pl.BlockSpec((LANES, LANES), lambda b, i: (0, 0)),
        ],
        out_shape=[
            jax.ShapeDtypeStruct((B, S, D // 2), U32),
            jax.ShapeDtypeStruct((B * nS, 4, tm), I32),
            jax.ShapeDtypeStruct((B * S, LANES), F32),
            jax.ShapeDtypeStruct((LANES, LANES), F32),
        ],
        scratch_shapes=[pltpu.VMEM((LANES, 1), F32)],
        compiler_params=_params(("arbitrary", "arbitrary")),
        name="moe_router",
    )(x, g.reshape(1, D), shift, scale, wr)
    return hp.reshape(B * S, D // 2), idx, gcol, cnt


def _row_copy(src, dst, sem):
    return pltpu.make_async_copy(src, dst, sem)


ROW_DMA_UNROLL = 8
ROW_WAIT_UNROLL = 64


def _wait_rows(n, src_row, dst_row, sem):
    def wait(i, c):
        for _ in range(ROW_WAIT_UNROLL):
            _row_copy(src_row, dst_row, sem).wait()
        return c

    assert n % ROW_WAIT_UNROLL == 0
    lax.fori_loop(0, n // ROW_WAIT_UNROLL, wait, 0)


def _dispatch_kernel(dest_ref, hp_ref, xs_in_ref, xs_ref, sem):
    del xs_in_ref
    tm = hp_ref.shape[0]

    def start(t, c):
        for k in range(2):
            _row_copy(hp_ref.at[pl.ds(t, 1)], xs_ref.at[pl.ds(dest_ref[0, k, t], 1)], sem).start()
        return c

    lax.fori_loop(0, tm, start, 0, unroll=ROW_DMA_UNROLL)
    _wait_rows(2 * tm, hp_ref.at[pl.ds(0, 1)], xs_ref.at[pl.ds(0, 1)], sem)


def moe_dispatch(dest, hp, n_slots):
    N, d2 = hp.shape
    nT, _, tm = dest.shape
    xs0 = jnp.zeros((n_slots, d2), U32)
    return pl.pallas_call(
        _dispatch_kernel,
        grid=(nT,),
        in_specs=[
            pl.BlockSpec((1, 2, tm), lambda t: (t, 0, 0), memory_space=pltpu.SMEM),
            pl.BlockSpec((tm, d2), lambda t: (t, 0)),
            pl.BlockSpec(memory_space=pl.ANY),
        ],
        out_specs=pl.BlockSpec(memory_space=pl.ANY),
        out_shape=jax.ShapeDtypeStruct((n_slots, d2), U32),
        scratch_shapes=[pltpu.SemaphoreType.DMA(())],
        input_output_aliases={2: 0},
        compiler_params=_params(("arbitrary",)),
        name="moe_dispatch",
    )(dest, hp, xs0)


def _moe_ffn_kernel(be_ref, nv_ref, xs_ref, wg_ref, wu_ref, wd_ref, y_ref, xb_ref):
    b = pl.program_id(0)
    j = pl.program_id(1)
    d2 = xs_ref.shape[1]

    @pl.when((b >= nv_ref[0]) & (j == 0))
    def _():
        y_ref[...] = jnp.zeros_like(y_ref)

    @pl.when(b < nv_ref[0])
    def _():
        @pl.when(j == 0)
        def _():
            w = xs_ref[...]
            xb_ref[:, :d2] = pltpu.bitcast(w << 16, F32).astype(BF16)
            xb_ref[:, d2:] = pltpu.bitcast(w & jnp.uint32(0xFFFF0000), F32).astype(BF16)

            y_ref[...] = jnp.zeros_like(y_ref)

        x = xb_ref[...]
        a = (jax.nn.silu(jnp.dot(x, wg_ref[0], preferred_element_type=F32))
             * jnp.dot(x, wu_ref[0], preferred_element_type=F32)).astype(BF16)
        _accumulate_down(y_ref, a, wd_ref.at[0])


def moe_expert_ffn(block_e, n_valid, xs, wg, wu, wd, *, tm, tf=512):
    n_slots, d2 = xs.shape
    D = 2 * d2
    FF = wg.shape[2]
    tf = _tile(FF, tf)
    nj = FF // tf
    nb = n_slots // tm

    def row_map(b, j, be, nv):
        return (jnp.minimum(b, nv[0] - 1), 0)

    def col_of(b, j, nv):
        return jnp.where(b < nv[0], j, nj - 1)

    grid_spec = pltpu.PrefetchScalarGridSpec(
        num_scalar_prefetch=2,
        grid=(nb, nj),
        in_specs=[
            pl.BlockSpec((tm, d2), row_map),
            pl.BlockSpec((1, D, tf), lambda b, j, be, nv: (be[b], 0, col_of(b, j, nv))),
            pl.BlockSpec((1, D, tf), lambda b, j, be, nv: (be[b], 0, col_of(b, j, nv))),
            pl.BlockSpec((1, tf, D), lambda b, j, be, nv: (be[b], col_of(b, j, nv), 0)),
        ],
        out_specs=pl.BlockSpec((tm, D), lambda b, j, be, nv: (b, 0)),
        scratch_shapes=[pltpu.VMEM((tm, D), BF16)],
    )
    return pl.pallas_call(
        _moe_ffn_kernel,
        grid_spec=grid_spec,
        out_shape=jax.ShapeDtypeStruct((n_slots, D), F32),
        compiler_params=_params(("arbitrary", "arbitrary")),
        name="moe_expert_ffn",
    )(block_e, n_valid, xs, wg, wu, wd)


def _combine_kernel(dest_ref, x_ref, gcol_ref, gate_ref, fg_ref, y_ref, o_ref, ya_ref, yb_ref, sem):
    tm = x_ref.shape[1]
    bufs = (ya_ref, yb_ref)

    def start(t, c):
        for k in range(2):
            _row_copy(y_ref.at[pl.ds(dest_ref[0, k, t], 1)], bufs[k].at[pl.ds(t, 1)], sem).start()
        return c

    lax.fori_loop(0, tm, start, 0, unroll=ROW_DMA_UNROLL)
    _wait_rows(2 * tm, y_ref.at[pl.ds(0, 1)], ya_ref.at[pl.ds(0, 1)], sem)

    moe = gcol_ref[:, 0:1] * ya_ref[...] + gcol_ref[:, 1:2] * yb_ref[...]
    x = x_ref[0] + gate_ref[0] * moe
    y = x * lax.rsqrt(jnp.mean(x * x, axis=-1, keepdims=True) + NORM_EPS)
    o_ref[0] = y * fg_ref[...]


def moe_combine(dest, x, gcol, gate, final_g, y_buf):
    B, S, D = x.shape
    nT, _, tm = dest.shape
    nS = S // tm
    return pl.pallas_call(
        _combine_kernel,
        grid=(B, nS),
        in_specs=[
            pl.BlockSpec((1, 2, tm), lambda b, i: (b * nS + i, 0, 0), memory_space=pltpu.SMEM),
            pl.BlockSpec((1, tm, D), lambda b, i: (b, i, 0)),
            pl.BlockSpec((tm, LANES), lambda b, i: (b * nS + i, 0)),
            pl.BlockSpec((1, 1, D), lambda b, i: (b, 0, 0)),
            pl.BlockSpec((1, D), lambda b, i: (0, 0)),
            pl.BlockSpec(memory_space=pl.ANY),
        ],
        out_specs=pl.BlockSpec((1, tm, D), lambda b, i: (b, i, 0)),
        out_shape=jax.ShapeDtypeStruct((B, S, D), F32),
        scratch_shapes=[pltpu.VMEM((tm, D), F32), pltpu.VMEM((tm, D), F32), pltpu.SemaphoreType.DMA(())],
        compiler_params=_params(("arbitrary", "arbitrary")),
        name="moe_combine",
    )(dest, x, gcol, gate, final_g.reshape(1, D), y_buf)


def moe_layout(cnt, idx, n_tokens, blk):
    counts = cnt[:N_EXPERTS, 0].astype(I32)
    padded = (counts + blk - 1) // blk * blk
    pad_end = jnp.cumsum(padded)
    pad_start = (pad_end - padded).astype(I32)
    n_blocks = (2 * n_tokens) // blk + N_EXPERTS
    n_valid = (pad_end[-1] // blk).astype(I32)
    starts = jnp.minimum(jnp.arange(n_blocks, dtype=I32), n_valid - 1) * blk
    block_e = jnp.minimum(jnp.sum(pad_end[None, :] <= starts[:, None], axis=1), N_EXPERTS - 1).astype(I32)
    e, pos = idx[:, 0:2, :], idx[:, 2:4, :]
    group_start = jnp.sum(jnp.where(e[..., None] == jnp.arange(N_EXPERTS, dtype=I32), pad_start, 0), axis=-1)
    dest = (pos + group_start).astype(I32)
    return dest, block_e, n_valid.reshape(1), n_blocks * blk


def rope_tables(n_tokens):
    t = jnp.arange(n_tokens)
    row = (t // GRID_W).astype(F32)
    col = (t % GRID_W).astype(F32)
    half = HEAD_DIM // 2
    freqs = ROPE_THETA ** (-jnp.arange(0, half, 2, dtype=F32) / half)
    ang = jnp.concatenate([row[:, None] * freqs, col[:, None] * freqs], axis=-1)
    cos = jnp.repeat(jnp.cos(ang), 2, axis=-1)
    sin = jnp.repeat(jnp.sin(ang), 2, axis=-1) * jnp.tile(jnp.array([-1.0, 1.0], F32), half)
    return cos, sin


def kernel(x, c, ctx, c_ctx, ada_w, ada_b, norm1_g, norm2_g, attn_w_in, attn_w_out, na_rpb, gqa_q_norm_g,
           gqa_k_norm_g, ffn_w_gate, ffn_w_up, ffn_w_down, conv_w_in, conv_w, conv_w_out, moe_w_router,
           moe_w_gate, moe_w_up, moe_w_down, final_norm_g):
    B, S, D = x.shape
    assert ada_w.shape[0] == 2 and S % GRID_W == 0 and S // GRID_W >= WIN_H
    kh = WIN_H

    n_cond = -(-(B + 1) // SUBLANES) * SUBLANES
    cond = jnp.zeros((n_cond, D), F32).at[:B].set(c).at[B].set(c_ctx)
    mod = adaln_terms(cond, ada_w, ada_b)

    def terms(layer, rows):
        m = mod[layer, rows].reshape(-1, 6, D)
        return [m[:, k][:, None, :] for k in range(6)]

    sh1, sc1, g1, sh2, sc2, g2 = terms(0, slice(0, B))
    csh1, csc1 = [jnp.broadcast_to(t, (B, 1, D)) for t in terms(0, slice(B, B + 1))[:2]]

    w_in = attn_w_in[0].astype(BF16)
    proj = norm_mod_matmul(x, norm1_g[0], sh1, sc1, w_in)
    kvc = norm_mod_matmul(ctx, norm1_g[0], csh1, csc1, w_in, col_off=Q_COLS, tm=256)
    cos, sin = rope_tables(S)
    qn = head_norm(proj, NA_COLS // GQA_Q_COLS, GQA_Q_HEADS, gqa_q_norm_g[0], cos, sin, rope=True,
                   out_scale=HEAD_DIM ** -0.5 * LOG2_E)
    kn = head_norm(proj, (Q_COLS + 2 * NA_COLS) // GQA_KV_COLS, GQA_KV_HEADS, gqa_k_norm_g[0], cos, sin, rope=True)
    L = ctx.shape[1]
    kcn = head_norm(kvc, 2 * NA_COLS // GQA_KV_COLS, GQA_KV_HEADS, gqa_k_norm_g[0], cos[:L], sin[:L], rope=False,
                    tm=L)
    out_a = neighborhood_attention(proj, kvc, na_bias_table(na_rpb[0], kh), kh)
    out_b = gqa_attention(qn, kn, proj, kcn, kvc)
    x = matmul_residual([out_a, out_b], attn_w_out[0].astype(BF16), x, g1)

    x = dense_ffn(x, norm2_g[0], sh2, sc2, g2, ffn_w_gate[0].astype(BF16), ffn_w_up[0].astype(BF16),
                  ffn_w_down[0].astype(BF16))

    sh1, sc1, g1, sh2, sc2, g2 = terms(1, slice(0, B))
    z = conv_mixer_inner(x, norm1_g[1], sh1, sc1, conv_w_in[0].astype(BF16), conv_w[0])
    x = matmul_residual([z], conv_w_out[0].astype(BF16), x, g1)

    moe_blk = 1024 if (2 * B * S) % 1024 == 0 else 2 * B * S // N_EXPERTS
    hp, idx, gcol, cnt = moe_router(x, norm2_g[1], sh2, sc2, moe_w_router[0])
    dest, block_e, n_valid, n_slots = moe_layout(cnt, idx, B * S, moe_blk)
    xs = moe_dispatch(dest, hp, n_slots)
    y_buf = moe_expert_ffn(block_e, n_valid, xs, moe_w_gate[0].astype(BF16), moe_w_up[0].astype(BF16),
                           moe_w_down[0].astype(BF16), tm=moe_blk)
    return moe_combine(dest, x, gcol, g2, final_norm_g, y_buf)
```

```python
import functools

import numpy as np
import jax
import jax.numpy as jnp
from jax import lax
from jax.experimental import pallas as pl
from jax.experimental.pallas import tpu as pltpu

F32 = jnp.float32
BF16 = jnp.bfloat16
I32 = jnp.int32
U32 = jnp.uint32

GRID_W = 64
HEAD_DIM = 128
NA_HEADS = 8
GQA_Q_HEADS = 8
GQA_KV_HEADS = 2
GQA_GROUP = GQA_Q_HEADS // GQA_KV_HEADS
WIN_H = 8
WIN_W = 16
ROPE_THETA = 10000.0
N_EXPERTS = 8
NORM_EPS = 1e-6
NA_COLS = NA_HEADS * HEAD_DIM
GQA_Q_COLS = GQA_Q_HEADS * HEAD_DIM
GQA_KV_COLS = GQA_KV_HEADS * HEAD_DIM
Q_COLS = NA_COLS + GQA_Q_COLS
KV_COLS = 2 * NA_COLS + 2 * GQA_KV_COLS

LANES = 128
SUBLANES = 8
V7X_VMEM_LIMIT = 56 * 1024 * 1024
MASK_VALUE = -1e30
NT_DIMS = (((1,), (1,)), ((), ()))
LOG2_E = 1.4426950408889634
GQA_UNSHIFTED_SCORE_LIMIT = 64.0
BF16_ROUNDING_MARGIN = 1.02


def _tile(dim, pref):
    return pref if dim % pref == 0 else dim


def _params(sem, vmem=V7X_VMEM_LIMIT):
    return pltpu.CompilerParams(dimension_semantics=sem, vmem_limit_bytes=vmem)


def _norm_mod(x, g, shift, scale):
    y = x * lax.rsqrt(jnp.mean(x * x, axis=-1, keepdims=True) + NORM_EPS)
    return (y * g) * (1.0 + scale) + shift


def _adaln_kernel(cond_ref, w_ref, b_ref, o_ref):
    a = jax.nn.silu(cond_ref[...])
    o_ref[0] = jnp.dot(a, w_ref[0], preferred_element_type=F32,
                       precision=lax.Precision.HIGHEST) + b_ref[0]


def adaln_terms(cond, ada_w, ada_b):
    L, D, N = ada_w.shape
    R = cond.shape[0]
    tn = _tile(N, 1024)
    return pl.pallas_call(
        _adaln_kernel,
        grid=(L, N // tn),
        in_specs=[
            pl.BlockSpec((R, D), lambda l, j: (0, 0)),
            pl.BlockSpec((1, D, tn), lambda l, j: (l, 0, j)),
            pl.BlockSpec((1, 1, tn), lambda l, j: (l, 0, j)),
        ],
        out_specs=pl.BlockSpec((1, R, tn), lambda l, j: (l, 0, j)),
        out_shape=jax.ShapeDtypeStruct((L, R, N), F32),
        compiler_params=_params(("parallel", "parallel")),
        name="adaln",
    )(cond, ada_w, ada_b.reshape(L, 1, N))


def _nm_matmul_kernel(x_ref, g_ref, sh_ref, sc_ref, w_ref, o_ref, h_ref):
    @pl.when(pl.program_id(2) == 0)
    def _():
        h_ref[...] = _norm_mod(x_ref[0], g_ref[...], sh_ref[0], sc_ref[0]).astype(BF16)

    o_ref[0] = jnp.dot(h_ref[...], w_ref[...], preferred_element_type=F32).astype(o_ref.dtype)


def norm_mod_matmul(x, g, shift, scale, w, *, col_off=0, n_cols=None, tm=1024, tn=512):
    B, S, D = x.shape
    n_cols = w.shape[1] - col_off if n_cols is None else n_cols
    tm = _tile(S, tm)
    tn = _tile(n_cols, tn)
    assert col_off % tn == 0
    joff = col_off // tn
    return pl.pallas_call(
        _nm_matmul_kernel,
        grid=(B, S // tm, n_cols // tn),
        in_specs=[
            pl.BlockSpec((1, tm, D), lambda b, i, j: (b, i, 0)),
            pl.BlockSpec((1, D), lambda b, i, j: (0, 0)),
            pl.BlockSpec((1, 1, D), lambda b, i, j: (b, 0, 0)),
            pl.BlockSpec((1, 1, D), lambda b, i, j: (b, 0, 0)),
            pl.BlockSpec((D, tn), lambda b, i, j: (0, j + joff)),
        ],
        out_specs=pl.BlockSpec((1, tm, tn), lambda b, i, j: (b, i, j)),
        out_shape=jax.ShapeDtypeStruct((B, S, n_cols), BF16),
        scratch_shapes=[pltpu.VMEM((tm, D), BF16)],
        compiler_params=_params(("parallel", "parallel", "arbitrary")),
        name="norm_mod_matmul",
    )(x, g.reshape(1, D), shift, scale, w)


def _head_norm_kernel(x_ref, g_ref, cos_ref, sin_ref, o_ref, *, n_heads, rope, out_scale):
    g = g_ref[...]
    for h in range(n_heads):
        sl = slice(h * HEAD_DIM, (h + 1) * HEAD_DIM)
        xh = x_ref[0, :, sl].astype(F32)
        y = xh * lax.rsqrt(jnp.mean(xh * xh, axis=-1, keepdims=True) + NORM_EPS) * g
        if rope:
            lane = lax.broadcasted_iota(I32, y.shape, 1)
            swap = jnp.where((lane & 1) == 0, pltpu.roll(y, HEAD_DIM - 1, 1), pltpu.roll(y, 1, 1))
            y = y * cos_ref[...] + swap * sin_ref[...]
        if out_scale != 1.0:
            y = y * out_scale
        o_ref[0, :, sl] = y.astype(o_ref.dtype)


def head_norm(src, col_block, n_heads, g, cos, sin, *, rope, out_scale=1.0, tm=512):
    B, S, _ = src.shape
    w = n_heads * HEAD_DIM
    tm = _tile(S, tm)
    kern = functools.partial(_head_norm_kernel, n_heads=n_heads, rope=rope, out_scale=out_scale)
    return pl.pallas_call(
        kern,
        grid=(B, S // tm),
        in_specs=[
            pl.BlockSpec((1, tm, w), lambda b, i: (b, i, col_block)),
            pl.BlockSpec((1, HEAD_DIM), lambda b, i: (0, 0)),
            pl.BlockSpec((tm, HEAD_DIM), lambda b, i: (i, 0)),
            pl.BlockSpec((tm, HEAD_DIM), lambda b, i: (i, 0)),
        ],
        out_specs=pl.BlockSpec((1, tm, w), lambda b, i: (b, i, 0)),
        out_shape=jax.ShapeDtypeStruct((B, S, w), BF16),
        compiler_params=_params(("parallel", "parallel")),
        name="head_norm",
    )(src, g.reshape(1, HEAD_DIM), cos, sin)


NA_GROUP = 4
NA_GROUP_UNROLL = 2
NA_VARIANTS = 3


def _na_kernel(q_ref, k_ref, v_ref, kc_ref, vc_ref, bias_ref, o_ref, *, rows, kh, scale):
    kc = kc_ref[0]
    vc = vc_ref[0]
    n_groups = rows // NA_GROUP
    win = NA_GROUP + kh
    gq = NA_GROUP * GRID_W

    def body(g, carry):
        w0 = jnp.clip(g * NA_GROUP - kh // 2, 0, rows - win)
        variant = jnp.where(g == 0, 0, jnp.where(g == n_groups - 1, 2, 1))
        qs = pl.ds(pl.multiple_of(g * gq, gq), gq)
        band = pl.ds(pl.multiple_of(w0 * GRID_W, GRID_W), win * GRID_W)
        q = q_ref[0, qs, :]
        kb = k_ref[0, band, :]
        vb = v_ref[0, band, :]
        s_loc = lax.dot_general(q, kb, NT_DIMS, preferred_element_type=F32) * scale + bias_ref[0, variant]
        s_ctx = lax.dot_general(q, kc, NT_DIMS, preferred_element_type=F32) * scale
        m = jnp.maximum(jnp.max(s_loc, axis=-1, keepdims=True), jnp.max(s_ctx, axis=-1, keepdims=True))
        p_loc = jnp.exp(s_loc - m)
        p_ctx = jnp.exp(s_ctx - m)
        l = jnp.sum(p_loc, axis=-1, keepdims=True) + jnp.sum(p_ctx, axis=-1, keepdims=True)
        o = (jnp.dot(p_ctx.astype(BF16), vc, preferred_element_type=F32)
             + jnp.dot(p_loc.astype(BF16), vb, preferred_element_type=F32))
        o_ref[0, qs, :] = (o / l).astype(o_ref.dtype)
        return carry

    lax.fori_loop(0, n_groups, body, 0, unroll=NA_GROUP_UNROLL)


def _bias_table_kernel(rpb_ref, row_sel_ref, col_sel_ref, valid_ref, o_ref):
    rows = jnp.dot(row_sel_ref[...], rpb_ref[0], preferred_element_type=F32, precision=lax.Precision.HIGHEST)
    tab = jnp.dot(rows, col_sel_ref[...], preferred_element_type=F32, precision=lax.Precision.HIGHEST)
    o_ref[0] = jnp.where(valid_ref[...] > 0.0, tab, MASK_VALUE)


def na_bias_table(rpb, kh):
    H, n_dr, n_dc = rpb.shape
    win = NA_GROUP + kh
    cols = np.arange(GRID_W)
    col_start = np.clip(cols - WIN_W // 2, 0, GRID_W - WIN_W)
    col_valid = (cols[None, :] >= col_start[:, None]) & (cols[None, :] < col_start[:, None] + WIN_W)
    dc_idx = np.clip(cols[None, :] - cols[:, None] + WIN_W - 1, 0, 2 * WIN_W - 2)
    ql = np.arange(NA_GROUP)[:, None]
    wr = np.arange(win)[None, :]
    band0 = [np.zeros_like(ql), ql, np.full_like(ql, win - kh)]
    dr = [wr - ql + WIN_H - 1, wr - ql + WIN_H - 1 - kh // 2, wr - ql + WIN_H - 1 - kh]
    pr = -(-(n_dr + 1) // SUBLANES) * SUBLANES
    pc = -(-n_dc // SUBLANES) * SUBLANES
    mask_row = pr - 1
    n_rows = NA_VARIANTS * NA_GROUP * win
    row_pick = np.concatenate([np.where((wr >= b0) & (wr < b0 + kh), d, mask_row).reshape(-1)
                               for b0, d in zip(band0, dr)])
    assert row_pick.min() >= 0 and (row_pick[row_pick != mask_row] < n_dr).all()
    row_sel = np.zeros((n_rows, pr), np.float32)
    row_sel[np.arange(n_rows), row_pick] = 1.0
    col_sel = np.zeros((pc, GRID_W * GRID_W), np.float32)
    col_sel[dc_idx.reshape(-1), np.arange(GRID_W * GRID_W)] = 1.0
    valid = col_valid.reshape(1, -1).astype(np.float32)
    rpb_p = jnp.zeros((H, pr, pc), F32).at[:, :n_dr, :n_dc].set(rpb.astype(F32)).at[:, mask_row, :].set(MASK_VALUE)
    tab = pl.pallas_call(
        _bias_table_kernel,
        grid=(H,),
        in_specs=[
            pl.BlockSpec((1, pr, pc), lambda h: (h, 0, 0)),
            pl.BlockSpec((n_rows, pr), lambda h: (0, 0)),
            pl.BlockSpec((pc, GRID_W * GRID_W), lambda h: (0, 0)),
            pl.BlockSpec((1, GRID_W * GRID_W), lambda h: (0, 0)),
        ],
        out_specs=pl.BlockSpec((1, n_rows, GRID_W * GRID_W), lambda h: (h, 0, 0)),
        out_shape=jax.ShapeDtypeStruct((H, n_rows, GRID_W * GRID_W), F32),
        compiler_params=_params(("parallel",)),
        name="na_bias_table",
    )(rpb_p, jnp.asarray(row_sel), jnp.asarray(col_sel), jnp.asarray(valid))
    tab = tab.reshape(H, NA_VARIANTS, NA_GROUP, win, GRID_W, GRID_W)
    return tab.transpose(0, 1, 2, 4, 3, 5).reshape(H, NA_VARIANTS, NA_GROUP * GRID_W, win * GRID_W)


def neighborhood_attention(proj, kvc, bias, kh):
    B, S, _ = proj.shape
    L = kvc.shape[1]
    rows = S // GRID_W
    assert rows % NA_GROUP == 0 and rows // NA_GROUP >= NA_VARIANTS
    qb, kb, vb = 0, Q_COLS // HEAD_DIM, (Q_COLS + NA_COLS) // HEAD_DIM
    kern = functools.partial(_na_kernel, rows=rows, kh=kh, scale=HEAD_DIM ** -0.5)
    return pl.pallas_call(
        kern,
        grid=(B, NA_HEADS),
        in_specs=[
            pl.BlockSpec((1, S, HEAD_DIM), lambda b, h: (b, 0, qb + h)),
            pl.BlockSpec((1, S, HEAD_DIM), lambda b, h: (b, 0, kb + h)),
            pl.BlockSpec((1, S, HEAD_DIM), lambda b, h: (b, 0, vb + h)),
            pl.BlockSpec((1, L, HEAD_DIM), lambda b, h: (b, 0, h)),
            pl.BlockSpec((1, L, HEAD_DIM), lambda b, h: (b, 0, NA_HEADS + h)),
            pl.BlockSpec((1,) + bias.shape[1:], lambda b, h: (h, 0, 0, 0)),
        ],
        out_specs=pl.BlockSpec((1, S, HEAD_DIM), lambda b, h: (b, 0, h)),
        out_shape=jax.ShapeDtypeStruct((B, S, NA_COLS), BF16),
        compiler_params=_params(("parallel", "parallel")),
        name="na_attention",
    )(proj, proj, proj, kvc, kvc, bias)


def _gqa_kernel(q_ref, k_ref, v_ref, kc_ref, vc_ref, o_ref, vx_ref, vcx_ref, *, subtract_max):
    @pl.when(pl.program_id(2) == 0)
    def _():
        vx_ref[:, :HEAD_DIM] = v_ref[0]
        vx_ref[:, HEAD_DIM:] = jnp.ones_like(v_ref[0])
        vcx_ref[:, :HEAD_DIM] = vc_ref[0]
        vcx_ref[:, HEAD_DIM:] = jnp.ones_like(vc_ref[0])

    k = k_ref[0]
    kc = kc_ref[0]
    vx = vx_ref[...]
    vcx = vcx_ref[...]
    for r in range(GQA_GROUP):
        sl = slice(r * HEAD_DIM, (r + 1) * HEAD_DIM)
        q = q_ref[0, :, sl]
        s_ctx = lax.dot_general(q, kc, NT_DIMS, preferred_element_type=F32)
        s_lat = lax.dot_general(q, k, NT_DIMS, preferred_element_type=F32)
        if subtract_max:
            m = jnp.maximum(jnp.max(s_ctx, axis=-1, keepdims=True), jnp.max(s_lat, axis=-1, keepdims=True))
            s_ctx = s_ctx - m
            s_lat = s_lat - m
        p_ctx = jnp.exp2(s_ctx).astype(BF16)
        p_lat = jnp.exp2(s_lat).astype(BF16)
        ol = (jnp.dot(p_ctx, vcx, preferred_element_type=F32) + jnp.dot(p_lat, vx, preferred_element_type=F32))
        o_ref[0, :, sl] = (ol[:, :HEAD_DIM] / ol[:, HEAD_DIM:]).astype(o_ref.dtype)


def gqa_attention(qn, kn, proj, kcn, kvc, *, subtract_max, tq=256):
    B, S, _ = qn.shape
    L = kcn.shape[1]
    tq = _tile(S, tq)
    gw = GQA_GROUP * HEAD_DIM
    vblk = (Q_COLS + 2 * NA_COLS + GQA_KV_COLS) // HEAD_DIM
    vcblk = (2 * NA_COLS + GQA_KV_COLS) // HEAD_DIM
    return pl.pallas_call(
        functools.partial(_gqa_kernel, subtract_max=subtract_max),
        grid=(B, GQA_KV_HEADS, S // tq),
        in_specs=[
            pl.BlockSpec((1, tq, gw), lambda b, g, i: (b, i, g)),
            pl.BlockSpec((1, S, HEAD_DIM), lambda b, g, i: (b, 0, g)),
            pl.BlockSpec((1, S, HEAD_DIM), lambda b, g, i: (b, 0, vblk + g)),
            pl.BlockSpec((1, L, HEAD_DIM), lambda b, g, i: (b, 0, g)),
            pl.BlockSpec((1, L, HEAD_DIM), lambda b, g, i: (b, 0, vcblk + g)),
        ],
        out_specs=pl.BlockSpec((1, tq, gw), lambda b, g, i: (b, i, g)),
        out_shape=jax.ShapeDtypeStruct((B, S, GQA_Q_COLS), BF16),
        scratch_shapes=[pltpu.VMEM((S, 2 * HEAD_DIM), BF16), pltpu.VMEM((L, 2 * HEAD_DIM), BF16)],
        compiler_params=_params(("parallel", "parallel", "arbitrary")),
        name="gqa_attention" if subtract_max else "gqa_attention_bounded",
    )(qn, kn, proj, kcn, kvc)


def _mm_res_kernel(*refs, k_sizes):
    n = len(k_sizes)
    w_ref, x_ref, gate_ref, o_ref = refs[n:]
    acc = None
    off = 0
    for a_ref, ks in zip(refs[:n], k_sizes):
        part = jnp.dot(a_ref[0], w_ref[off:off + ks, :], preferred_element_type=F32)
        acc = part if acc is None else acc + part
        off += ks
    o_ref[0] = x_ref[0] + gate_ref[0] * acc


def matmul_residual(lhs_list, w, x, gate, *, tm=512, tn=2048):
    B, S, N = x.shape
    K = w.shape[0]
    k_sizes = tuple(a.shape[-1] for a in lhs_list)
    assert sum(k_sizes) == K
    tm = _tile(S, tm)
    tn = _tile(N, tn)
    lhs_specs = [pl.BlockSpec((1, tm, ks), lambda b, i, j: (b, i, 0)) for ks in k_sizes]
    return pl.pallas_call(
        functools.partial(_mm_res_kernel, k_sizes=k_sizes),
        grid=(B, S // tm, N // tn),
        in_specs=lhs_specs + [
            pl.BlockSpec((K, tn), lambda b, i, j: (0, j)),
            pl.BlockSpec((1, tm, tn), lambda b, i, j: (b, i, j)),
            pl.BlockSpec((1, 1, tn), lambda b, i, j: (b, 0, j)),
        ],
        out_specs=pl.BlockSpec((1, tm, tn), lambda b, i, j: (b, i, j)),
        out_shape=jax.ShapeDtypeStruct((B, S, N), F32),
        compiler_params=_params(("parallel", "parallel", "arbitrary")),
        name="matmul_residual",
    )(*lhs_list, w, x, gate)


DOWN_CHUNK = 512


def _accumulate_down(acc_ref, a, wd_ref):
    n = acc_ref.shape[1]
    step = DOWN_CHUNK if n % DOWN_CHUNK == 0 else n
    for n0 in range(0, n, step):
        acc_ref[:, n0:n0 + step] += jnp.dot(a, wd_ref[:, n0:n0 + step], preferred_element_type=F32)


def _ffn_kernel(x_ref, g_ref, sh_ref, sc_ref, gate_ref, wg_ref, wu_ref, wd_ref, o_ref, h_ref, acc_ref):
    j = pl.program_id(2)

    @pl.when(j == 0)
    def _():
        h_ref[...] = _norm_mod(x_ref[0], g_ref[...], sh_ref[0], sc_ref[0]).astype(BF16)
        acc_ref[...] = jnp.zeros_like(acc_ref)

    h = h_ref[...]
    a = (jax.nn.silu(jnp.dot(h, wg_ref[...], preferred_element_type=F32))
         * jnp.dot(h, wu_ref[...], preferred_element_type=F32)).astype(BF16)
    _accumulate_down(acc_ref, a, wd_ref)

    @pl.when(j == pl.num_programs(2) - 1)
    def _():
        o_ref[0] = x_ref[0] + gate_ref[0] * acc_ref[...]


def dense_ffn(x, g, shift, scale, gate, wg, wu, wd, *, tm=512, tf=512):
    B, S, D = x.shape
    FF = wg.shape[1]
    tm = _tile(S, tm)
    tf = _tile(FF, tf)
    vec = pl.BlockSpec((1, 1, D), lambda b, i, j: (b, 0, 0))
    return pl.pallas_call(
        _ffn_kernel,
        grid=(B, S // tm, FF // tf),
        in_specs=[
            pl.BlockSpec((1, tm, D), lambda b, i, j: (b, i, 0)),
            pl.BlockSpec((1, D), lambda b, i, j: (0, 0)),
            vec, vec, vec,
            pl.BlockSpec((D, tf), lambda b, i, j: (0, j)),
            pl.BlockSpec((D, tf), lambda b, i, j: (0, j)),
            pl.BlockSpec((tf, D), lambda b, i, j: (j, 0)),
        ],
        out_specs=pl.BlockSpec((1, tm, D), lambda b, i, j: (b, i, 0)),
        out_shape=jax.ShapeDtypeStruct((B, S, D), F32),
        scratch_shapes=[pltpu.VMEM((tm, D), BF16), pltpu.VMEM((tm, D), F32)],
        compiler_params=_params(("parallel", "parallel", "arbitrary")),
        name="dense_ffn",
    )(x, g.reshape(1, D), shift, scale, gate, wg, wu, wd)


def _conv_kernel(x_ref, xp_ref, xn_ref, g_ref, sh_ref, sc_ref, wb_ref, wc_ref, wx_ref, cw_ref,
                 o_ref, h_ref):
    i = pl.program_id(1)
    tm = x_ref.shape[1]

    @pl.when(pl.program_id(2) == 0)
    def _():
        g, sh, sc = g_ref[...], sh_ref[0], sc_ref[0]
        h_ref[:tm, :] = _norm_mod(x_ref[0], g, sh, sc).astype(BF16)
        halo = jnp.concatenate([xp_ref[0], xn_ref[0]], axis=0)
        h_ref[tm:, :] = _norm_mod(halo, g, sh, sc).astype(BF16)

    h = h_ref[...]
    bg = jnp.dot(h_ref[:tm, :], wb_ref[...], preferred_element_type=F32)
    ux = (jnp.dot(h, wc_ref[...], preferred_element_type=F32)
          * jnp.dot(h, wx_ref[...], preferred_element_type=F32))
    u = ux[:tm]
    uh = ux[tm:]
    prev = jnp.where(i > 0, uh[SUBLANES - 1:SUBLANES, :], 0.0)
    nxt = jnp.where(i < pl.num_programs(1) - 1, uh[SUBLANES:SUBLANES + 1, :], 0.0)
    row = lax.broadcasted_iota(I32, u.shape, 0)
    up = jnp.where(row == 0, prev, pltpu.roll(u, 1, 0))
    dn = jnp.where(row == tm - 1, nxt, pltpu.roll(u, tm - 1, 0))
    cw = cw_ref[...]
    y = cw[0:1, :] * up + cw[1:2, :] * u + cw[2:3, :] * dn
    o_ref[0] = (bg * y).astype(o_ref.dtype)


def conv_mixer_inner(x, g, shift, scale, w_in, conv_w, *, tm=1024, tc=512):
    B, S, D = x.shape
    tm = _tile(S, tm)
    tc = _tile(D, tc)
    nc = D // tc
    hb = tm // SUBLANES
    last_hb = S // SUBLANES - 1
    vec = pl.BlockSpec((1, 1, D), lambda b, i, j: (b, 0, 0))
    return pl.pallas_call(
        _conv_kernel,
        grid=(B, S // tm, nc),
        in_specs=[
            pl.BlockSpec((1, tm, D), lambda b, i, j: (b, i, 0)),
            pl.BlockSpec((1, SUBLANES, D), lambda b, i, j: (b, jnp.maximum(i * hb - 1, 0), 0)),
            pl.BlockSpec((1, SUBLANES, D), lambda b, i, j: (b, jnp.minimum((i + 1) * hb, last_hb), 0)),
            pl.BlockSpec((1, D), lambda b, i, j: (0, 0)),
            vec, vec,
            pl.BlockSpec((D, tc), lambda b, i, j: (0, j)),
            pl.BlockSpec((D, tc), lambda b, i, j: (0, nc + j)),
            pl.BlockSpec((D, tc), lambda b, i, j: (0, 2 * nc + j)),
            pl.BlockSpec((3, tc), lambda b, i, j: (0, j)),
        ],
        out_specs=pl.BlockSpec((1, tm, tc), lambda b, i, j: (b, i, j)),
        out_shape=jax.ShapeDtypeStruct((B, S, D), BF16),
        scratch_shapes=[pltpu.VMEM((tm + 2 * SUBLANES, D), BF16)],
        compiler_params=_params(("parallel", "parallel", "arbitrary")),
        name="conv_mixer",
    )(x, x, x, g.reshape(1, D), shift, scale, w_in, w_in, w_in, conv_w)


def _router_kernel(x_ref, g_ref, sh_ref, sc_ref, wr_ref, hp_ref, idx_ref, gcol_ref, cnt_ref, run_ref):
    tm, D = x_ref.shape[1], x_ref.shape[2]
    d2 = D // 2

    @pl.when((pl.program_id(0) == 0) & (pl.program_id(1) == 0))
    def _():
        run_ref[...] = jnp.zeros_like(run_ref)

    h = _norm_mod(x_ref[0], g_ref[...], sh_ref[0], sc_ref[0])

    lo = pltpu.bitcast(h[:, :d2].astype(BF16).astype(F32), U32)
    hi = pltpu.bitcast(h[:, d2:].astype(BF16).astype(F32), U32)
    hp_ref[0] = (hi & jnp.uint32(0xFFFF0000)) | (lo >> 16)

    logits = jnp.dot(h, wr_ref[...], preferred_element_type=F32, precision=lax.Precision.HIGHEST)
    lane = lax.broadcasted_iota(I32, logits.shape, 1).astype(F32)
    neg = jnp.float32(-jnp.inf)
    l1 = jnp.where(lane < N_EXPERTS, logits, neg)
    m1 = jnp.max(l1, axis=-1, keepdims=True)
    i1 = jnp.min(jnp.where(l1 == m1, lane, float(LANES)), axis=-1, keepdims=True)
    l2 = jnp.where(lane == i1, neg, l1)
    m2 = jnp.max(l2, axis=-1, keepdims=True)
    i2 = jnp.min(jnp.where(l2 == m2, lane, float(LANES)), axis=-1, keepdims=True)
    e = jnp.exp(m2 - m1)
    den = 1.0 + e
    gcol_ref[...] = jnp.where(lane == 0, 1.0 / den, jnp.where(lane == 1, e / den, 0.0))

    chosen = jnp.where((lane == i1) | (lane == i2), 1.0, 0.0)
    chosen_t = chosen.T
    rr = lax.broadcasted_iota(I32, (tm, tm), 0)
    cc = lax.broadcasted_iota(I32, (tm, tm), 1)
    before = jnp.where(rr < cc, 1.0, 0.0).astype(BF16)
    cum = jnp.dot(chosen_t.astype(BF16), before, preferred_element_type=F32) + run_ref[...]
    sel = jnp.where(lane == 0, i1, jnp.where(lane == 1, i2, 0.0)).T
    i1r = sel[0:1, :]
    i2r = sel[1:2, :]
    sub = lax.broadcasted_iota(I32, cum.shape, 0).astype(F32)
    p1r = jnp.sum(jnp.where(sub == i1r, cum, 0.0), axis=0, keepdims=True)
    p2r = jnp.sum(jnp.where(sub == i2r, cum, 0.0), axis=0, keepdims=True)
    idx_ref[0] = jnp.concatenate([i1r, i2r, p1r, p2r], axis=0).astype(I32)

    run_ref[...] += jnp.sum(chosen_t, axis=1, keepdims=True)
    cnt_ref[...] = jnp.broadcast_to(run_ref[...], cnt_ref.shape)


def moe_router(x, g, shift, scale, w_router, *, tm=512):
    B, S, D = x.shape
    E = w_router.shape[1]
    tm = _tile(S, tm)
    nS = S // tm
    wr = jnp.zeros((D, LANES), F32).at[:, :E].set(w_router.astype(F32))
    vec = pl.BlockSpec((1, 1, D), lambda b, i: (b, 0, 0))
    hp, idx, gcol, cnt = pl.pallas_call(
        _router_kernel,
        grid=(B, nS),
        in_specs=[
            pl.BlockSpec((1, tm, D), lambda b, i: (b, i, 0)),
            pl.BlockSpec((1, D), lambda b, i: (0, 0)),
            vec, vec,
            pl.BlockSpec((D, LANES), lambda b, i: (0, 0)),
        ],
        out_specs=[
            pl.BlockSpec((1, tm, D // 2), lambda b, i: (b, i, 0)),
            pl.BlockSpec((1, 4, tm), lambda b, i: (b * nS + i, 0, 0)),
            pl.BlockSpec((tm, LANES), lambda b, i: (b * nS + i, 0)),
            pl.BlockSpec((LANES, LANES), lambda b, i: (0, 0)),
        ],
        out_shape=[
            jax.ShapeDtypeStruct((B, S, D // 2), U32),
            jax.ShapeDtypeStruct((B * nS, 4, tm), I32),
            jax.ShapeDtypeStruct((B * S, LANES), F32),
            jax.ShapeDtypeStruct((LANES, LANES), F32),
        ],
        scratch_shapes=[pltpu.VMEM((LANES, 1), F32)],
        compiler_params=_params(("arbitrary", "arbitrary")),
        name="moe_router",
    )(x, g.reshape(1, D), shift, scale, wr)
    return hp.reshape(B * S, D // 2), idx, gcol, cnt


def _row_copy(src, dst, sem):
    return pltpu.make_async_copy(src, dst, sem)


ROW_DMA_UNROLL = 8
ROW_WAIT_UNROLL = 64


def _wait_rows(n, src_row, dst_row, sem):
    def wait(i, c):
        for _ in range(ROW_WAIT_UNROLL):
            _row_copy(src_row, dst_row, sem).wait()
        return c

    assert n % ROW_WAIT_UNROLL == 0
    lax.fori_loop(0, n // ROW_WAIT_UNROLL, wait, 0)


def _dispatch_kernel(dest_ref, hp_ref, xs_in_ref, xs_ref, sem):
    del xs_in_ref
    tm = hp_ref.shape[0]

    def start(t, c):
        for k in range(2):
            _row_copy(hp_ref.at[pl.ds(t, 1)], xs_ref.at[pl.ds(dest_ref[0, k, t], 1)], sem).start()
        return c

    lax.fori_loop(0, tm, start, 0, unroll=ROW_DMA_UNROLL)
    _wait_rows(2 * tm, hp_ref.at[pl.ds(0, 1)], xs_ref.at[pl.ds(0, 1)], sem)


def moe_dispatch(dest, hp, n_slots):
    N, d2 = hp.shape
    nT, _, tm = dest.shape
    xs0 = jnp.zeros((n_slots, d2), U32)
    return pl.pallas_call(
        _dispatch_kernel,
        grid=(nT,),
        in_specs=[
            pl.BlockSpec((1, 2, tm), lambda t: (t, 0, 0), memory_space=pltpu.SMEM),
            pl.BlockSpec((tm, d2), lambda t: (t, 0)),
            pl.BlockSpec(memory_space=pl.ANY),
        ],
        out_specs=pl.BlockSpec(memory_space=pl.ANY),
        out_shape=jax.ShapeDtypeStruct((n_slots, d2), U32),
        scratch_shapes=[pltpu.SemaphoreType.DMA(())],
        input_output_aliases={2: 0},
        compiler_params=_params(("arbitrary",)),
        name="moe_dispatch",
    )(dest, hp, xs0)


def _moe_ffn_kernel(be_ref, nv_ref, xs_ref, wg_ref, wu_ref, wd_ref, y_ref, xb_ref):
    b = pl.program_id(0)
    j = pl.program_id(1)
    d2 = xs_ref.shape[1]

    @pl.when((b >= nv_ref[0]) & (j == 0))
    def _():
        y_ref[...] = jnp.zeros_like(y_ref)

    @pl.when(b < nv_ref[0])
    def _():
        @pl.when(j == 0)
        def _():
            w = xs_ref[...]
            xb_ref[:, :d2] = pltpu.bitcast(w << 16, F32).astype(BF16)
            xb_ref[:, d2:] = pltpu.bitcast(w & jnp.uint32(0xFFFF0000), F32).astype(BF16)

            y_ref[...] = jnp.zeros_like(y_ref)

        x = xb_ref[...]
        a = (jax.nn.silu(jnp.dot(x, wg_ref[0], preferred_element_type=F32))
             * jnp.dot(x, wu_ref[0], preferred_element_type=F32)).astype(BF16)
        _accumulate_down(y_ref, a, wd_ref.at[0])


def moe_expert_ffn(block_e, n_valid, xs, wg, wu, wd, *, tm, tf=512):
    n_slots, d2 = xs.shape
    D = 2 * d2
    FF = wg.shape[2]
    tf = _tile(FF, tf)
    nj = FF // tf
    nb = n_slots // tm

    def row_map(b, j, be, nv):
        return (jnp.minimum(b, nv[0] - 1), 0)

    def col_of(b, j, nv):
        return jnp.where(b < nv[0], j, nj - 1)

    grid_spec = pltpu.PrefetchScalarGridSpec(
        num_scalar_prefetch=2,
        grid=(nb, nj),
        in_specs=[
            pl.BlockSpec((tm, d2), row_map),
            pl.BlockSpec((1, D, tf), lambda b, j, be, nv: (be[b], 0, col_of(b, j, nv))),
            pl.BlockSpec((1, D, tf), lambda b, j, be, nv: (be[b], 0, col_of(b, j, nv))),
            pl.BlockSpec((1, tf, D), lambda b, j, be, nv: (be[b], col_of(b, j, nv), 0)),
        ],
        out_specs=pl.BlockSpec((tm, D), lambda b, j, be, nv: (b, 0)),
        scratch_shapes=[pltpu.VMEM((tm, D), BF16)],
    )
    return pl.pallas_call(
        _moe_ffn_kernel,
        grid_spec=grid_spec,
        out_shape=jax.ShapeDtypeStruct((n_slots, D), F32),
        compiler_params=_params(("arbitrary", "arbitrary")),
        name="moe_expert_ffn",
    )(block_e, n_valid, xs, wg, wu, wd)


def _combine_kernel(dest_ref, x_ref, gcol_ref, gate_ref, fg_ref, y_ref, o_ref, ya_ref, yb_ref, sem):
    tm = x_ref.shape[1]
    bufs = (ya_ref, yb_ref)

    def start(t, c):
        for k in range(2):
            _row_copy(y_ref.at[pl.ds(dest_ref[0, k, t], 1)], bufs[k].at[pl.ds(t, 1)], sem).start()
        return c

    lax.fori_loop(0, tm, start, 0, unroll=ROW_DMA_UNROLL)
    _wait_rows(2 * tm, y_ref.at[pl.ds(0, 1)], ya_ref.at[pl.ds(0, 1)], sem)

    moe = gcol_ref[:, 0:1] * ya_ref[...] + gcol_ref[:, 1:2] * yb_ref[...]
    x = x_ref[0] + gate_ref[0] * moe
    y = x * lax.rsqrt(jnp.mean(x * x, axis=-1, keepdims=True) + NORM_EPS)
    o_ref[0] = y * fg_ref[...]


def moe_combine(dest, x, gcol, gate, final_g, y_buf):
    B, S, D = x.shape
    nT, _, tm = dest.shape
    nS = S // tm
    return pl.pallas_call(
        _combine_kernel,
        grid=(B, nS),
        in_specs=[
            pl.BlockSpec((1, 2, tm), lambda b, i: (b * nS + i, 0, 0), memory_space=pltpu.SMEM),
            pl.BlockSpec((1, tm, D), lambda b, i: (b, i, 0)),
            pl.BlockSpec((tm, LANES), lambda b, i: (b * nS + i, 0)),
            pl.BlockSpec((1, 1, D), lambda b, i: (b, 0, 0)),
            pl.BlockSpec((1, D), lambda b, i: (0, 0)),
            pl.BlockSpec(memory_space=pl.ANY),
        ],
        out_specs=pl.BlockSpec((1, tm, D), lambda b, i: (b, i, 0)),
        out_shape=jax.ShapeDtypeStruct((B, S, D), F32),
        scratch_shapes=[pltpu.VMEM((tm, D), F32), pltpu.VMEM((tm, D), F32), pltpu.SemaphoreType.DMA(())],
        compiler_params=_params(("arbitrary", "arbitrary")),
        name="moe_combine",
    )(dest, x, gcol, gate, final_g.reshape(1, D), y_buf)


def moe_layout(cnt, idx, n_tokens, blk):
    counts = cnt[:N_EXPERTS, 0].astype(I32)
    padded = (counts + blk - 1) // blk * blk
    pad_end = jnp.cumsum(padded)
    pad_start = (pad_end - padded).astype(I32)
    n_blocks = (2 * n_tokens) // blk + N_EXPERTS
    n_valid = (pad_end[-1] // blk).astype(I32)
    starts = jnp.minimum(jnp.arange(n_blocks, dtype=I32), n_valid - 1) * blk
    block_e = jnp.minimum(jnp.sum(pad_end[None, :] <= starts[:, None], axis=1), N_EXPERTS - 1).astype(I32)
    e, pos = idx[:, 0:2, :], idx[:, 2:4, :]
    group_start = jnp.sum(jnp.where(e[..., None] == jnp.arange(N_EXPERTS, dtype=I32), pad_start, 0), axis=-1)
    dest = (pos + group_start).astype(I32)
    return dest, block_e, n_valid.reshape(1), n_blocks * blk


def rope_tables(n_tokens):
    t = jnp.arange(n_tokens)
    row = (t // GRID_W).astype(F32)
    col = (t % GRID_W).astype(F32)
    half = HEAD_DIM // 2
    freqs = ROPE_THETA ** (-jnp.arange(0, half, 2, dtype=F32) / half)
    ang = jnp.concatenate([row[:, None] * freqs, col[:, None] * freqs], axis=-1)
    cos = jnp.repeat(jnp.cos(ang), 2, axis=-1)
    sin = jnp.repeat(jnp.sin(ang), 2, axis=-1) * jnp.tile(jnp.array([-1.0, 1.0], F32), half)
    return cos, sin


def kernel(x, c, ctx, c_ctx, ada_w, ada_b, norm1_g, norm2_g, attn_w_in, attn_w_out, na_rpb, gqa_q_norm_g,
           gqa_k_norm_g, ffn_w_gate, ffn_w_up, ffn_w_down, conv_w_in, conv_w, conv_w_out, moe_w_router,
           moe_w_gate, moe_w_up, moe_w_down, final_norm_g):
    B, S, D = x.shape
    assert ada_w.shape[0] == 2 and S % GRID_W == 0 and S // GRID_W >= WIN_H
    kh = WIN_H

    n_cond = -(-(B + 1) // SUBLANES) * SUBLANES
    cond = jnp.zeros((n_cond, D), F32).at[:B].set(c).at[B].set(c_ctx)
    mod = adaln_terms(cond, ada_w, ada_b)

    def terms(layer, rows):
        m = mod[layer, rows].reshape(-1, 6, D)
        return [m[:, k][:, None, :] for k in range(6)]

    sh1, sc1, g1, sh2, sc2, g2 = terms(0, slice(0, B))
    csh1, csc1 = [jnp.broadcast_to(t, (B, 1, D)) for t in terms(0, slice(B, B + 1))[:2]]

    w_in = attn_w_in[0].astype(BF16)
    proj = norm_mod_matmul(x, norm1_g[0], sh1, sc1, w_in)
    kvc = norm_mod_matmul(ctx, norm1_g[0], csh1, csc1, w_in, col_off=Q_COLS, tm=256)
    cos, sin = rope_tables(S)
    q_scale = HEAD_DIM ** -0.5 * LOG2_E
    qn = head_norm(proj, NA_COLS // GQA_Q_COLS, GQA_Q_HEADS, gqa_q_norm_g[0], cos, sin, rope=True,
                   out_scale=q_scale)
    kn = head_norm(proj, (Q_COLS + 2 * NA_COLS) // GQA_KV_COLS, GQA_KV_HEADS, gqa_k_norm_g[0], cos, sin, rope=True)
    L = ctx.shape[1]
    kcn = head_norm(kvc, 2 * NA_COLS // GQA_KV_COLS, GQA_KV_HEADS, gqa_k_norm_g[0], cos[:L], sin[:L], rope=False,
                    tm=L)
    out_a = neighborhood_attention(proj, kvc, na_bias_table(na_rpb[0], kh), kh)
    score_bound = (HEAD_DIM * q_scale * BF16_ROUNDING_MARGIN
                   * jnp.max(jnp.abs(gqa_q_norm_g[0])) * jnp.max(jnp.abs(gqa_k_norm_g[0])))
    out_b = lax.cond(
        score_bound <= GQA_UNSHIFTED_SCORE_LIMIT,
        functools.partial(gqa_attention, subtract_max=False),
        functools.partial(gqa_attention, subtract_max=True),
        qn, kn, proj, kcn, kvc)
    x = matmul_residual([out_a, out_b], attn_w_out[0].astype(BF16), x, g1)

    x = dense_ffn(x, norm2_g[0], sh2, sc2, g2, ffn_w_gate[0].astype(BF16), ffn_w_up[0].astype(BF16),
                  ffn_w_down[0].astype(BF16))

    sh1, sc1, g1, sh2, sc2, g2 = terms(1, slice(0, B))
    z = conv_mixer_inner(x, norm1_g[1], sh1, sc1, conv_w_in[0].astype(BF16), conv_w[0])
    x = matmul_residual([z], conv_w_out[0].astype(BF16), x, g1)

    moe_blk = 1024 if (2 * B * S) % 1024 == 0 else 2 * B * S // N_EXPERTS
    hp, idx, gcol, cnt = moe_router(x, norm2_g[1], sh2, sc2, moe_w_router[0])
    dest, block_e, n_valid, n_slots = moe_layout(cnt, idx, B * S, moe_blk)
    xs = moe_dispatch(dest, hp, n_slots)
    y_buf = moe_expert_ffn(block_e, n_valid, xs, moe_w_gate[0].astype(BF16), moe_w_up[0].astype(BF16),
                           moe_w_down[0].astype(BF16), tm=moe_blk)
    return moe_combine(dest, x, gcol, g2, final_norm_g, y_buf)
```

```python
import functools

import numpy as np
import jax
import jax.numpy as jnp
from jax import lax
from jax.experimental import pallas as pl
from jax.experimental.pallas import tpu as pltpu

F32 = jnp.float32
BF16 = jnp.bfloat16
I32 = jnp.int32
U32 = jnp.uint32

GRID_W = 64
HEAD_DIM = 128
NA_HEADS = 8
GQA_Q_HEADS = 8
GQA_KV_HEADS = 2
GQA_GROUP = GQA_Q_HEADS // GQA_KV_HEADS
WIN_H = 8
WIN_W = 16
ROPE_THETA = 10000.0
N_EXPERTS = 8
NORM_EPS = 1e-6
NA_COLS = NA_HEADS * HEAD_DIM
GQA_Q_COLS = GQA_Q_HEADS * HEAD_DIM
GQA_KV_COLS = GQA_KV_HEADS * HEAD_DIM
Q_COLS = NA_COLS + GQA_Q_COLS
KV_COLS = 2 * NA_COLS + 2 * GQA_KV_COLS

LANES = 128
SUBLANES = 8
V7X_VMEM_LIMIT = 56 * 1024 * 1024
MASK_VALUE = -1e30
NT_DIMS = (((1,), (1,)), ((), ()))
LOG2_E = 1.4426950408889634
GQA_UNSHIFTED_SCORE_LIMIT = 64.0
BF16_ROUNDING_MARGIN = 1.02


def _tile(dim, pref):
    return pref if dim % pref == 0 else dim


def _params(sem, vmem=V7X_VMEM_LIMIT):
    return pltpu.CompilerParams(dimension_semantics=sem, vmem_limit_bytes=vmem)


def _norm_mod(x, g, shift, scale):
    y = x * lax.rsqrt(jnp.mean(x * x, axis=-1, keepdims=True) + NORM_EPS)
    return (y * g) * (1.0 + scale) + shift


def _adaln_kernel(cond_ref, w_ref, b_ref, o_ref):
    a = jax.nn.silu(cond_ref[...])
    o_ref[0] = jnp.dot(a, w_ref[0], preferred_element_type=F32,
                       precision=lax.Precision.HIGHEST) + b_ref[0]


def adaln_terms(cond, ada_w, ada_b):
    L, D, N = ada_w.shape
    R = cond.shape[0]
    tn = _tile(N, 1024)
    return pl.pallas_call(
        _adaln_kernel,
        grid=(L, N // tn),
        in_specs=[
            pl.BlockSpec((R, D), lambda l, j: (0, 0)),
            pl.BlockSpec((1, D, tn), lambda l, j: (l, 0, j)),
            pl.BlockSpec((1, 1, tn), lambda l, j: (l, 0, j)),
        ],
        out_specs=pl.BlockSpec((1, R, tn), lambda l, j: (l, 0, j)),
        out_shape=jax.ShapeDtypeStruct((L, R, N), F32),
        compiler_params=_params(("parallel", "parallel")),
        name="adaln",
    )(cond, ada_w, ada_b.reshape(L, 1, N))


BF16_SUBLANES = 16


class SideCast:
    def __init__(self, arrays, grid):
        self.arrays = list(arrays)
        self.grid = grid
        B, nI, nJ = grid
        self.plans = []
        for a in self.arrays:
            R = a.shape[0]
            fits = [nj for nj in range(nJ, 0, -1)
                    if R % (B * nI * nj) == 0 and (R // (B * nI * nj)) % BF16_SUBLANES == 0]
            assert fits, "no even split of the rows over the grid"
            self.plans.append((fits[0], R // (B * nI * fits[0])))

    @staticmethod
    def fits(arrays, grid):
        try:
            SideCast(arrays, grid)
            return True
        except AssertionError:
            return False

    def specs(self):
        _, nI, _ = self.grid

        def spec(a, nj, rows):
            return pl.BlockSpec((rows, a.shape[1]), lambda b, i, j: ((b * nI + i) * nj + jnp.minimum(j, nj - 1), 0))

        return [spec(a, nj, rows) for a, (nj, rows) in zip(self.arrays, self.plans)]

    def out_shapes(self):
        return [jax.ShapeDtypeStruct(a.shape, BF16) for a in self.arrays]

    def run(self, src_refs, dst_refs):
        j = pl.program_id(2)
        for src, dst, (nj, _) in zip(src_refs, dst_refs, self.plans):
            @pl.when(j < nj)
            def _():
                dst[...] = src[...].astype(BF16)


def _nm_matmul_kernel(*refs, side):
    n = len(side.arrays) if side else 0
    x_ref, g_ref, sh_ref, sc_ref, w_ref = refs[:5]
    o_ref = refs[5 + n]
    h_ref = refs[-1]

    @pl.when(pl.program_id(2) == 0)
    def _():
        h_ref[...] = _norm_mod(x_ref[0], g_ref[...], sh_ref[0], sc_ref[0]).astype(BF16)

    o_ref[0] = jnp.dot(h_ref[...], w_ref[...], preferred_element_type=F32).astype(o_ref.dtype)
    if side:
        side.run(refs[5:5 + n], refs[6 + n:6 + 2 * n])


def norm_mod_matmul(x, g, shift, scale, w, *, col_off=0, n_cols=None, tm=1024, tn=512, side_cast=()):
    B, S, D = x.shape
    n_cols = w.shape[1] - col_off if n_cols is None else n_cols
    tm = _tile(S, tm)
    tn = _tile(n_cols, tn)
    assert col_off % tn == 0
    joff = col_off // tn
    grid = (B, S // tm, n_cols // tn)
    side = SideCast(side_cast, grid) if side_cast else None
    out = pl.pallas_call(
        functools.partial(_nm_matmul_kernel, side=side),
        grid=grid,
        in_specs=[
            pl.BlockSpec((1, tm, D), lambda b, i, j: (b, i, 0)),
            pl.BlockSpec((1, D), lambda b, i, j: (0, 0)),
            pl.BlockSpec((1, 1, D), lambda b, i, j: (b, 0, 0)),
            pl.BlockSpec((1, 1, D), lambda b, i, j: (b, 0, 0)),
            pl.BlockSpec((D, tn), lambda b, i, j: (0, j + joff)),
        ] + (side.specs() if side else []),
        out_specs=[pl.BlockSpec((1, tm, tn), lambda b, i, j: (b, i, j))] + (side.specs() if side else []),
        out_shape=[jax.ShapeDtypeStruct((B, S, n_cols), BF16)] + (side.out_shapes() if side else []),
        scratch_shapes=[pltpu.VMEM((tm, D), BF16)],
        compiler_params=_params(("parallel", "parallel", "arbitrary")),
        name="norm_mod_matmul",
    )(x, g.reshape(1, D), shift, scale, w, *side_cast)
    return out if side else out[0]


def _head_norm_kernel(x_ref, g_ref, cos_ref, sin_ref, o_ref, *, n_heads, rope, out_scale):
    g = g_ref[...]
    for h in range(n_heads):
        sl = slice(h * HEAD_DIM, (h + 1) * HEAD_DIM)
        xh = x_ref[0, :, sl].astype(F32)
        y = xh * lax.rsqrt(jnp.mean(xh * xh, axis=-1, keepdims=True) + NORM_EPS) * g
        if rope:
            lane = lax.broadcasted_iota(I32, y.shape, 1)
            swap = jnp.where((lane & 1) == 0, pltpu.roll(y, HEAD_DIM - 1, 1), pltpu.roll(y, 1, 1))
            y = y * cos_ref[...] + swap * sin_ref[...]
        if out_scale != 1.0:
            y = y * out_scale
        o_ref[0, :, sl] = y.astype(o_ref.dtype)


def head_norm(src, col_block, n_heads, g, cos, sin, *, rope, out_scale=1.0, tm=512):
    B, S, _ = src.shape
    w = n_heads * HEAD_DIM
    tm = _tile(S, tm)
    kern = functools.partial(_head_norm_kernel, n_heads=n_heads, rope=rope, out_scale=out_scale)
    return pl.pallas_call(
        kern,
        grid=(B, S // tm),
        in_specs=[
            pl.BlockSpec((1, tm, w), lambda b, i: (b, i, col_block)),
            pl.BlockSpec((1, HEAD_DIM), lambda b, i: (0, 0)),
            pl.BlockSpec((tm, HEAD_DIM), lambda b, i: (i, 0)),
            pl.BlockSpec((tm, HEAD_DIM), lambda b, i: (i, 0)),
        ],
        out_specs=pl.BlockSpec((1, tm, w), lambda b, i: (b, i, 0)),
        out_shape=jax.ShapeDtypeStruct((B, S, w), BF16),
        compiler_params=_params(("parallel", "parallel")),
        name="head_norm",
    )(src, g.reshape(1, HEAD_DIM), cos, sin)


NA_GROUP = 4
NA_GROUP_UNROLL = 2
NA_VARIANTS = 3


def _na_kernel(q_ref, k_ref, v_ref, kc_ref, vc_ref, bias_ref, o_ref, *, rows, kh, scale):
    kc = kc_ref[0]
    vc = vc_ref[0]
    n_groups = rows // NA_GROUP
    win = NA_GROUP + kh
    gq = NA_GROUP * GRID_W

    def body(g, carry):
        w0 = jnp.clip(g * NA_GROUP - kh // 2, 0, rows - win)
        variant = jnp.where(g == 0, 0, jnp.where(g == n_groups - 1, 2, 1))
        qs = pl.ds(pl.multiple_of(g * gq, gq), gq)
        band = pl.ds(pl.multiple_of(w0 * GRID_W, GRID_W), win * GRID_W)
        q = q_ref[0, qs, :]
        kb = k_ref[0, band, :]
        vb = v_ref[0, band, :]
        s_loc = lax.dot_general(q, kb, NT_DIMS, preferred_element_type=F32) * scale + bias_ref[0, variant]
        s_ctx = lax.dot_general(q, kc, NT_DIMS, preferred_element_type=F32) * scale
        m = jnp.maximum(jnp.max(s_loc, axis=-1, keepdims=True), jnp.max(s_ctx, axis=-1, keepdims=True))
        p_loc = jnp.exp(s_loc - m)
        p_ctx = jnp.exp(s_ctx - m)
        l = jnp.sum(p_loc, axis=-1, keepdims=True) + jnp.sum(p_ctx, axis=-1, keepdims=True)
        o = (jnp.dot(p_ctx.astype(BF16), vc, preferred_element_type=F32)
             + jnp.dot(p_loc.astype(BF16), vb, preferred_element_type=F32))
        o_ref[0, qs, :] = (o / l).astype(o_ref.dtype)
        return carry

    lax.fori_loop(0, n_groups, body, 0, unroll=NA_GROUP_UNROLL)


def _bias_table_kernel(rpb_ref, row_sel_ref, col_sel_ref, valid_ref, o_ref):
    rows = jnp.dot(row_sel_ref[...], rpb_ref[0], preferred_element_type=F32, precision=lax.Precision.HIGHEST)
    tab = jnp.dot(rows, col_sel_ref[...], preferred_element_type=F32, precision=lax.Precision.HIGHEST)
    o_ref[0] = jnp.where(valid_ref[...] > 0.0, tab, MASK_VALUE)


def na_bias_table(rpb, kh):
    H, n_dr, n_dc = rpb.shape
    win = NA_GROUP + kh
    cols = np.arange(GRID_W)
    col_start = np.clip(cols - WIN_W // 2, 0, GRID_W - WIN_W)
    col_valid = (cols[None, :] >= col_start[:, None]) & (cols[None, :] < col_start[:, None] + WIN_W)
    dc_idx = np.clip(cols[None, :] - cols[:, None] + WIN_W - 1, 0, 2 * WIN_W - 2)
    ql = np.arange(NA_GROUP)[:, None]
    wr = np.arange(win)[None, :]
    band0 = [np.zeros_like(ql), ql, np.full_like(ql, win - kh)]
    dr = [wr - ql + WIN_H - 1, wr - ql + WIN_H - 1 - kh // 2, wr - ql + WIN_H - 1 - kh]
    pr = -(-(n_dr + 1) // SUBLANES) * SUBLANES
    pc = -(-n_dc // SUBLANES) * SUBLANES
    mask_row = pr - 1
    n_rows = NA_VARIANTS * NA_GROUP * win
    row_pick = np.concatenate([np.where((wr >= b0) & (wr < b0 + kh), d, mask_row).reshape(-1)
                               for b0, d in zip(band0, dr)])
    assert row_pick.min() >= 0 and (row_pick[row_pick != mask_row] < n_dr).all()
    row_sel = np.zeros((n_rows, pr), np.float32)
    row_sel[np.arange(n_rows), row_pick] = 1.0
    col_sel = np.zeros((pc, GRID_W * GRID_W), np.float32)
    col_sel[dc_idx.reshape(-1), np.arange(GRID_W * GRID_W)] = 1.0
    valid = col_valid.reshape(1, -1).astype(np.float32)
    rpb_p = jnp.zeros((H, pr, pc), F32).at[:, :n_dr, :n_dc].set(rpb.astype(F32)).at[:, mask_row, :].set(MASK_VALUE)
    tab = pl.pallas_call(
        _bias_table_kernel,
        grid=(H,),
        in_specs=[
            pl.BlockSpec((1, pr, pc), lambda h: (h, 0, 0)),
            pl.BlockSpec((n_rows, pr), lambda h: (0, 0)),
            pl.BlockSpec((pc, GRID_W * GRID_W), lambda h: (0, 0)),
            pl.BlockSpec((1, GRID_W * GRID_W), lambda h: (0, 0)),
        ],
        out_specs=pl.BlockSpec((1, n_rows, GRID_W * GRID_W), lambda h: (h, 0, 0)),
        out_shape=jax.ShapeDtypeStruct((H, n_rows, GRID_W * GRID_W), F32),
        compiler_params=_params(("parallel",)),
        name="na_bias_table",
    )(rpb_p, jnp.asarray(row_sel), jnp.asarray(col_sel), jnp.asarray(valid))
    tab = tab.reshape(H, NA_VARIANTS, NA_GROUP, win, GRID_W, GRID_W)
    return tab.transpose(0, 1, 2, 4, 3, 5).reshape(H, NA_VARIANTS, NA_GROUP * GRID_W, win * GRID_W)


def neighborhood_attention(proj, kvc, bias, kh):
    B, S, _ = proj.shape
    L = kvc.shape[1]
    rows = S // GRID_W
    assert rows % NA_GROUP == 0 and rows // NA_GROUP >= NA_VARIANTS
    qb, kb, vb = 0, Q_COLS // HEAD_DIM, (Q_COLS + NA_COLS) // HEAD_DIM
    kern = functools.partial(_na_kernel, rows=rows, kh=kh, scale=HEAD_DIM ** -0.5)
    return pl.pallas_call(
        kern,
        grid=(B, NA_HEADS),
        in_specs=[
            pl.BlockSpec((1, S, HEAD_DIM), lambda b, h: (b, 0, qb + h)),
            pl.BlockSpec((1, S, HEAD_DIM), lambda b, h: (b, 0, kb + h)),
            pl.BlockSpec((1, S, HEAD_DIM), lambda b, h: (b, 0, vb + h)),
            pl.BlockSpec((1, L, HEAD_DIM), lambda b, h: (b, 0, h)),
            pl.BlockSpec((1, L, HEAD_DIM), lambda b, h: (b, 0, NA_HEADS + h)),
            pl.BlockSpec((1,) + bias.shape[1:], lambda b, h: (h, 0, 0, 0)),
        ],
        out_specs=pl.BlockSpec((1, S, HEAD_DIM), lambda b, h: (b, 0, h)),
        out_shape=jax.ShapeDtypeStruct((B, S, NA_COLS), BF16),
        compiler_params=_params(("parallel", "parallel")),
        name="na_attention",
    )(proj, proj, proj, kvc, kvc, bias)


def _gqa_kernel(q_ref, k_ref, v_ref, kc_ref, vc_ref, o_ref, vx_ref, vcx_ref, *, subtract_max):
    @pl.when(pl.program_id(2) == 0)
    def _():
        vx_ref[:, :HEAD_DIM] = v_ref[0]
        vx_ref[:, HEAD_DIM:] = jnp.ones_like(v_ref[0])
        vcx_ref[:, :HEAD_DIM] = vc_ref[0]
        vcx_ref[:, HEAD_DIM:] = jnp.ones_like(vc_ref[0])

    k = k_ref[0]
    kc = kc_ref[0]
    vx = vx_ref[...]
    vcx = vcx_ref[...]
    for r in range(GQA_GROUP):
        sl = slice(r * HEAD_DIM, (r + 1) * HEAD_DIM)
        q = q_ref[0, :, sl]
        s_ctx = lax.dot_general(q, kc, NT_DIMS, preferred_element_type=F32)
        s_lat = lax.dot_general(q, k, NT_DIMS, preferred_element_type=F32)
        if subtract_max:
            m = jnp.maximum(jnp.max(s_ctx, axis=-1, keepdims=True), jnp.max(s_lat, axis=-1, keepdims=True))
            s_ctx = s_ctx - m
            s_lat = s_lat - m
        p_ctx = jnp.exp2(s_ctx).astype(BF16)
        p_lat = jnp.exp2(s_lat).astype(BF16)
        ol = (jnp.dot(p_ctx, vcx, preferred_element_type=F32) + jnp.dot(p_lat, vx, preferred_element_type=F32))
        o_ref[0, :, sl] = (ol[:, :HEAD_DIM] / ol[:, HEAD_DIM:]).astype(o_ref.dtype)


def gqa_attention(qn, kn, proj, kcn, kvc, *, subtract_max, tq=256):
    B, S, _ = qn.shape
    L = kcn.shape[1]
    tq = _tile(S, tq)
    gw = GQA_GROUP * HEAD_DIM
    vblk = (Q_COLS + 2 * NA_COLS + GQA_KV_COLS) // HEAD_DIM
    vcblk = (2 * NA_COLS + GQA_KV_COLS) // HEAD_DIM
    return pl.pallas_call(
        functools.partial(_gqa_kernel, subtract_max=subtract_max),
        grid=(B, GQA_KV_HEADS, S // tq),
        in_specs=[
            pl.BlockSpec((1, tq, gw), lambda b, g, i: (b, i, g)),
            pl.BlockSpec((1, S, HEAD_DIM), lambda b, g, i: (b, 0, g)),
            pl.BlockSpec((1, S, HEAD_DIM), lambda b, g, i: (b, 0, vblk + g)),
            pl.BlockSpec((1, L, HEAD_DIM), lambda b, g, i: (b, 0, g)),
            pl.BlockSpec((1, L, HEAD_DIM), lambda b, g, i: (b, 0, vcblk + g)),
        ],
        out_specs=pl.BlockSpec((1, tq, gw), lambda b, g, i: (b, i, g)),
        out_shape=jax.ShapeDtypeStruct((B, S, GQA_Q_COLS), BF16),
        scratch_shapes=[pltpu.VMEM((S, 2 * HEAD_DIM), BF16), pltpu.VMEM((L, 2 * HEAD_DIM), BF16)],
        compiler_params=_params(("parallel", "parallel", "arbitrary")),
        name="gqa_attention" if subtract_max else "gqa_attention_bounded",
    )(qn, kn, proj, kcn, kvc)


def _mm_res_kernel(*refs, k_sizes):
    n = len(k_sizes)
    w_ref, x_ref, gate_ref, o_ref = refs[n:]
    acc = None
    off = 0
    for a_ref, ks in zip(refs[:n], k_sizes):
        part = jnp.dot(a_ref[0], w_ref[off:off + ks, :], preferred_element_type=F32)
        acc = part if acc is None else acc + part
        off += ks
    o_ref[0] = x_ref[0] + gate_ref[0] * acc


def matmul_residual(lhs_list, w, x, gate, *, tm=512, tn=2048):
    B, S, N = x.shape
    K = w.shape[0]
    k_sizes = tuple(a.shape[-1] for a in lhs_list)
    assert sum(k_sizes) == K
    tm = _tile(S, tm)
    tn = _tile(N, tn)
    lhs_specs = [pl.BlockSpec((1, tm, ks), lambda b, i, j: (b, i, 0)) for ks in k_sizes]
    return pl.pallas_call(
        functools.partial(_mm_res_kernel, k_sizes=k_sizes),
        grid=(B, S // tm, N // tn),
        in_specs=lhs_specs + [
            pl.BlockSpec((K, tn), lambda b, i, j: (0, j)),
            pl.BlockSpec((1, tm, tn), lambda b, i, j: (b, i, j)),
            pl.BlockSpec((1, 1, tn), lambda b, i, j: (b, 0, j)),
        ],
        out_specs=pl.BlockSpec((1, tm, tn), lambda b, i, j: (b, i, j)),
        out_shape=jax.ShapeDtypeStruct((B, S, N), F32),
        compiler_params=_params(("parallel", "parallel", "arbitrary")),
        name="matmul_residual",
    )(*lhs_list, w, x, gate)


DOWN_CHUNK = 512


def _accumulate_down(acc_ref, a, wd_ref):
    n = acc_ref.shape[1]
    step = DOWN_CHUNK if n % DOWN_CHUNK == 0 else n
    for n0 in range(0, n, step):
        acc_ref[:, n0:n0 + step] += jnp.dot(a, wd_ref[:, n0:n0 + step], preferred_element_type=F32)


def _ffn_kernel(*refs, side):
    n = len(side.arrays) if side else 0
    x_ref, g_ref, sh_ref, sc_ref, gate_ref, wg_ref, wu_ref, wd_ref = refs[:8]
    o_ref = refs[8 + n]
    h_ref, acc_ref = refs[-2:]
    j = pl.program_id(2)
    if side:
        side.run(refs[8:8 + n], refs[9 + n:9 + 2 * n])

    @pl.when(j == 0)
    def _():
        h_ref[...] = _norm_mod(x_ref[0], g_ref[...], sh_ref[0], sc_ref[0]).astype(BF16)
        acc_ref[...] = jnp.zeros_like(acc_ref)

    h = h_ref[...]
    a = (jax.nn.silu(jnp.dot(h, wg_ref[...], preferred_element_type=F32))
         * jnp.dot(h, wu_ref[...], preferred_element_type=F32)).astype(BF16)
    _accumulate_down(acc_ref, a, wd_ref)

    @pl.when(j == pl.num_programs(2) - 1)
    def _():
        o_ref[0] = x_ref[0] + gate_ref[0] * acc_ref[...]


def dense_ffn_grid(x_shape, ff, tm=512, tf=512):
    B, S, _ = x_shape
    return (B, S // _tile(S, tm), ff // _tile(ff, tf))


def dense_ffn(x, g, shift, scale, gate, wg, wu, wd, *, tm=512, tf=512, side_cast=()):
    B, S, D = x.shape
    FF = wg.shape[1]
    tm = _tile(S, tm)
    tf = _tile(FF, tf)
    grid = dense_ffn_grid(x.shape, FF, tm, tf)
    side = SideCast(side_cast, grid) if side_cast else None
    vec = pl.BlockSpec((1, 1, D), lambda b, i, j: (b, 0, 0))
    out = pl.pallas_call(
        functools.partial(_ffn_kernel, side=side),
        grid=grid,
        in_specs=[
            pl.BlockSpec((1, tm, D), lambda b, i, j: (b, i, 0)),
            pl.BlockSpec((1, D), lambda b, i, j: (0, 0)),
            vec, vec, vec,
            pl.BlockSpec((D, tf), lambda b, i, j: (0, j)),
            pl.BlockSpec((D, tf), lambda b, i, j: (0, j)),
            pl.BlockSpec((tf, D), lambda b, i, j: (j, 0)),
        ] + (side.specs() if side else []),
        out_specs=[pl.BlockSpec((1, tm, D), lambda b, i, j: (b, i, 0))] + (side.specs() if side else []),
        out_shape=[jax.ShapeDtypeStruct((B, S, D), F32)] + (side.out_shapes() if side else []),
        scratch_shapes=[pltpu.VMEM((tm, D), BF16), pltpu.VMEM((tm, D), F32)],
        compiler_params=_params(("parallel", "parallel", "arbitrary")),
        name="dense_ffn",
    )(x, g.reshape(1, D), shift, scale, gate, wg, wu, wd, *side_cast)
    return out if side else out[0]


def _conv_kernel(x_ref, xp_ref, xn_ref, g_ref, sh_ref, sc_ref, wb_ref, wc_ref, wx_ref, cw_ref,
                 o_ref, h_ref):
    i = pl.program_id(1)
    tm = x_ref.shape[1]

    @pl.when(pl.program_id(2) == 0)
    def _():
        g, sh, sc = g_ref[...], sh_ref[0], sc_ref[0]
        h_ref[:tm, :] = _norm_mod(x_ref[0], g, sh, sc).astype(BF16)
        halo = jnp.concatenate([xp_ref[0], xn_ref[0]], axis=0)
        h_ref[tm:, :] = _norm_mod(halo, g, sh, sc).astype(BF16)

    h = h_ref[...]
    bg = jnp.dot(h_ref[:tm, :], wb_ref[...], preferred_element_type=F32)
    ux = (jnp.dot(h, wc_ref[...], preferred_element_type=F32)
          * jnp.dot(h, wx_ref[...], preferred_element_type=F32))
    u = ux[:tm]
    uh = ux[tm:]
    prev = jnp.where(i > 0, uh[SUBLANES - 1:SUBLANES, :], 0.0)
    nxt = jnp.where(i < pl.num_programs(1) - 1, uh[SUBLANES:SUBLANES + 1, :], 0.0)
    row = lax.broadcasted_iota(I32, u.shape, 0)
    up = jnp.where(row == 0, prev, pltpu.roll(u, 1, 0))
    dn = jnp.where(row == tm - 1, nxt, pltpu.roll(u, tm - 1, 0))
    cw = cw_ref[...]
    y = cw[0:1, :] * up + cw[1:2, :] * u + cw[2:3, :] * dn
    o_ref[0] = (bg * y).astype(o_ref.dtype)


def conv_mixer_inner(x, g, shift, scale, w_in, conv_w, *, tm=1024, tc=512):
    B, S, D = x.shape
    tm = _tile(S, tm)
    tc = _tile(D, tc)
    nc = D // tc
    hb = tm // SUBLANES
    last_hb = S // SUBLANES - 1
    vec = pl.BlockSpec((1, 1, D), lambda b, i, j: (b, 0, 0))
    return pl.pallas_call(
        _conv_kernel,
        grid=(B, S // tm, nc),
        in_specs=[
            pl.BlockSpec((1, tm, D), lambda b, i, j: (b, i, 0)),
            pl.BlockSpec((1, SUBLANES, D), lambda b, i, j: (b, jnp.maximum(i * hb - 1, 0), 0)),
            pl.BlockSpec((1, SUBLANES, D), lambda b, i, j: (b, jnp.minimum((i + 1) * hb, last_hb), 0)),
            pl.BlockSpec((1, D), lambda b, i, j: (0, 0)),
            vec, vec,
            pl.BlockSpec((D, tc), lambda b, i, j: (0, j)),
            pl.BlockSpec((D, tc), lambda b, i, j: (0, nc + j)),
            pl.BlockSpec((D, tc), lambda b, i, j: (0, 2 * nc + j)),
            pl.BlockSpec((3, tc), lambda b, i, j: (0, j)),
        ],
        out_specs=pl.BlockSpec((1, tm, tc), lambda b, i, j: (b, i, j)),
        out_shape=jax.ShapeDtypeStruct((B, S, D), BF16),
        scratch_shapes=[pltpu.VMEM((tm + 2 * SUBLANES, D), BF16)],
        compiler_params=_params(("parallel", "parallel", "arbitrary")),
        name="conv_mixer",
    )(x, x, x, g.reshape(1, D), shift, scale, w_in, w_in, w_in, conv_w)


def _router_kernel(x_ref, g_ref, sh_ref, sc_ref, wr_ref, hp_ref, idx_ref, gcol_ref, cnt_ref, run_ref):
    tm, D = x_ref.shape[1], x_ref.shape[2]
    d2 = D // 2

    @pl.when((pl.program_id(0) == 0) & (pl.program_id(1) == 0))
    def _():
        run_ref[...] = jnp.zeros_like(run_ref)

    h = _norm_mod(x_ref[0], g_ref[...], sh_ref[0], sc_ref[0])

    lo = pltpu.bitcast(h[:, :d2].astype(BF16).astype(F32), U32)
    hi = pltpu.bitcast(h[:, d2:].astype(BF16).astype(F32), U32)
    hp_ref[0] = (hi & jnp.uint32(0xFFFF0000)) | (lo >> 16)

    logits = jnp.dot(h, wr_ref[...], preferred_element_type=F32, precision=lax.Precision.HIGHEST)
    lane = lax.broadcasted_iota(I32, logits.shape, 1).astype(F32)
    neg = jnp.float32(-jnp.inf)
    l1 = jnp.where(lane < N_EXPERTS, logits, neg)
    m1 = jnp.max(l1, axis=-1, keepdims=True)
    i1 = jnp.min(jnp.where(l1 == m1, lane, float(LANES)), axis=-1, keepdims=True)
    l2 = jnp.where(lane == i1, neg, l1)
    m2 = jnp.max(l2, axis=-1, keepdims=True)
    i2 = jnp.min(jnp.where(l2 == m2, lane, float(LANES)), axis=-1, keepdims=True)
    e = jnp.exp(m2 - m1)
    den = 1.0 + e
    gcol_ref[...] = jnp.where(lane == 0, 1.0 / den, jnp.where(lane == 1, e / den, 0.0))

    chosen = jnp.where((lane == i1) | (lane == i2), 1.0, 0.0)
    chosen_t = chosen.T
    rr = lax.broadcasted_iota(I32, (tm, tm), 0)
    cc = lax.broadcasted_iota(I32, (tm, tm), 1)
    before = jnp.where(rr < cc, 1.0, 0.0).astype(BF16)
    cum = jnp.dot(chosen_t.astype(BF16), before, preferred_element_type=F32) + run_ref[...]
    sel = jnp.where(lane == 0, i1, jnp.where(lane == 1, i2, 0.0)).T
    i1r = sel[0:1, :]
    i2r = sel[1:2, :]
    sub = lax.broadcasted_iota(I32, cum.shape, 0).astype(F32)
    p1r = jnp.sum(jnp.where(sub == i1r, cum, 0.0), axis=0, keepdims=True)
    p2r = jnp.sum(jnp.where(sub == i2r, cum, 0.0), axis=0, keepdims=True)
    idx_ref[0] = jnp.concatenate([i1r, i2r, p1r, p2r], axis=0).astype(I32)

    run_ref[...] += jnp.sum(chosen_t, axis=1, keepdims=True)
    cnt_ref[...] = jnp.broadcast_to(run_ref[...], cnt_ref.shape)


def moe_router(x, g, shift, scale, w_router, *, tm=512):
    B, S, D = x.shape
    E = w_router.shape[1]
    tm = _tile(S, tm)
    nS = S // tm
    wr = jnp.zeros((D, LANES), F32).at[:, :E].set(w_router.astype(F32))
    vec = pl.BlockSpec((1, 1, D), lambda b, i: (b, 0, 0))
    hp, idx, gcol, cnt = pl.pallas_call(
        _router_kernel,
        grid=(B, nS),
        in_specs=[
            pl.BlockSpec((1, tm, D), lambda b, i: (b, i, 0)),
            pl.BlockSpec((1, D), lambda b, i: (0, 0)),
            vec, vec,
            pl.BlockSpec((D, LANES), lambda b, i: (0, 0)),
        ],
        out_specs=[
            pl.BlockSpec((1, tm, D // 2), lambda b, i: (b, i, 0)),
            pl.BlockSpec((1, 4, tm), lambda b, i: (b * nS + i, 0, 0)),
            pl.BlockSpec((tm, LANES), lambda b, i: (b * nS + i, 0)),
            pl.BlockSpec((LANES, LANES), lambda b, i: (0, 0)),
        ],
        out_shape=[
            jax.ShapeDtypeStruct((B, S, D // 2), U32),
            jax.ShapeDtypeStruct((B * nS, 4, tm), I32),
            jax.ShapeDtypeStruct((B * S, LANES), F32),
            jax.ShapeDtypeStruct((LANES, LANES), F32),
        ],
        scratch_shapes=[pltpu.VMEM((LANES, 1), F32)],
        compiler_params=_params(("arbitrary", "arbitrary")),
        name="moe_router",
    )(x, g.reshape(1, D), shift, scale, wr)
    return hp.reshape(B * S, D // 2), idx, gcol, cnt


def _row_copy(src, dst, sem):
    return pltpu.make_async_copy(src, dst, sem)


ROW_DMA_UNROLL = 8
ROW_WAIT_UNROLL = 64


def _wait_rows(n, src_row, dst_row, sem):
    def wait(i, c):
        for _ in range(ROW_WAIT_UNROLL):
            _row_copy(src_row, dst_row, sem).wait()
        return c

    assert n % ROW_WAIT_UNROLL == 0
    lax.fori_loop(0, n // ROW_WAIT_UNROLL, wait, 0)


def _dispatch_kernel(dest_ref, hp_ref, xs_in_ref, xs_ref, sem):
    del xs_in_ref
    tm = hp_ref.shape[0]

    def start(t, c):
        for k in range(2):
            _row_copy(hp_ref.at[pl.ds(t, 1)], xs_ref.at[pl.ds(dest_ref[0, k, t], 1)], sem).start()
        return c

    lax.fori_loop(0, tm, start, 0, unroll=ROW_DMA_UNROLL)
    _wait_rows(2 * tm, hp_ref.at[pl.ds(0, 1)], xs_ref.at[pl.ds(0, 1)], sem)


def moe_dispatch(dest, hp, n_slots):
    N, d2 = hp.shape
    nT, _, tm = dest.shape
    xs0 = jnp.zeros((n_slots, d2), U32)
    return pl.pallas_call(
        _dispatch_kernel,
        grid=(nT,),
        in_specs=[
            pl.BlockSpec((1, 2, tm), lambda t: (t, 0, 0), memory_space=pltpu.SMEM),
            pl.BlockSpec((tm, d2), lambda t: (t, 0)),
            pl.BlockSpec(memory_space=pl.ANY),
        ],
        out_specs=pl.BlockSpec(memory_space=pl.ANY),
        out_shape=jax.ShapeDtypeStruct((n_slots, d2), U32),
        scratch_shapes=[pltpu.SemaphoreType.DMA(())],
        input_output_aliases={2: 0},
        compiler_params=_params(("arbitrary",)),
        name="moe_dispatch",
    )(dest, hp, xs0)


def _moe_ffn_kernel(be_ref, nv_ref, xs_ref, wg_ref, wu_ref, wd_ref, y_ref, xb_ref):
    b = pl.program_id(0)
    j = pl.program_id(1)
    d2 = xs_ref.shape[1]

    @pl.when((b >= nv_ref[0]) & (j == 0))
    def _():
        y_ref[...] = jnp.zeros_like(y_ref)

    @pl.when(b < nv_ref[0])
    def _():
        @pl.when(j == 0)
        def _():
            w = xs_ref[...]
            xb_ref[:, :d2] = pltpu.bitcast(w << 16, F32).astype(BF16)
            xb_ref[:, d2:] = pltpu.bitcast(w & jnp.uint32(0xFFFF0000), F32).astype(BF16)

            y_ref[...] = jnp.zeros_like(y_ref)

        x = xb_ref[...]
        a = (jax.nn.silu(jnp.dot(x, wg_ref[0], preferred_element_type=F32))
             * jnp.dot(x, wu_ref[0], preferred_element_type=F32)).astype(BF16)
        _accumulate_down(y_ref, a, wd_ref.at[0])


def moe_expert_ffn(block_e, n_valid, xs, wg, wu, wd, *, tm, tf=512):
    n_slots, d2 = xs.shape
    D = 2 * d2
    FF = wg.shape[2]
    tf = _tile(FF, tf)
    nj = FF // tf
    nb = n_slots // tm

    def row_map(b, j, be, nv):
        return (jnp.minimum(b, nv[0] - 1), 0)

    def col_of(b, j, nv):
        return jnp.where(b < nv[0], j, nj - 1)

    grid_spec = pltpu.PrefetchScalarGridSpec(
        num_scalar_prefetch=2,
        grid=(nb, nj),
        in_specs=[
            pl.BlockSpec((tm, d2), row_map),
            pl.BlockSpec((1, D, tf), lambda b, j, be, nv: (be[b], 0, col_of(b, j, nv))),
            pl.BlockSpec((1, D, tf), lambda b, j, be, nv: (be[b], 0, col_of(b, j, nv))),
            pl.BlockSpec((1, tf, D), lambda b, j, be, nv: (be[b], col_of(b, j, nv), 0)),
        ],
        out_specs=pl.BlockSpec((tm, D), lambda b, j, be, nv: (b, 0)),
        scratch_shapes=[pltpu.VMEM((tm, D), BF16)],
    )
    return pl.pallas_call(
        _moe_ffn_kernel,
        grid_spec=grid_spec,
        out_shape=jax.ShapeDtypeStruct((n_slots, D), F32),
        compiler_params=_params(("arbitrary", "arbitrary")),
        name="moe_expert_ffn",
    )(block_e, n_valid, xs, wg, wu, wd)


def _combine_kernel(dest_ref, x_ref, gcol_ref, gate_ref, fg_ref, y_ref, o_ref, ya_ref, yb_ref, sem):
    tm = x_ref.shape[1]
    bufs = (ya_ref, yb_ref)

    def start(t, c):
        for k in range(2):
            _row_copy(y_ref.at[pl.ds(dest_ref[0, k, t], 1)], bufs[k].at[pl.ds(t, 1)], sem).start()
        return c

    lax.fori_loop(0, tm, start, 0, unroll=ROW_DMA_UNROLL)
    _wait_rows(2 * tm, y_ref.at[pl.ds(0, 1)], ya_ref.at[pl.ds(0, 1)], sem)

    moe = gcol_ref[:, 0:1] * ya_ref[...] + gcol_ref[:, 1:2] * yb_ref[...]
    x = x_ref[0] + gate_ref[0] * moe
    y = x * lax.rsqrt(jnp.mean(x * x, axis=-1, keepdims=True) + NORM_EPS)
    o_ref[0] = y * fg_ref[...]


def moe_combine(dest, x, gcol, gate, final_g, y_buf):
    B, S, D = x.shape
    nT, _, tm = dest.shape
    nS = S // tm
    return pl.pallas_call(
        _combine_kernel,
        grid=(B, nS),
        in_specs=[
            pl.BlockSpec((1, 2, tm), lambda b, i: (b * nS + i, 0, 0), memory_space=pltpu.SMEM),
            pl.BlockSpec((1, tm, D), lambda b, i: (b, i, 0)),
            pl.BlockSpec((tm, LANES), lambda b, i: (b * nS + i, 0)),
            pl.BlockSpec((1, 1, D), lambda b, i: (b, 0, 0)),
            pl.BlockSpec((1, D), lambda b, i: (0, 0)),
            pl.BlockSpec(memory_space=pl.ANY),
        ],
        out_specs=pl.BlockSpec((1, tm, D), lambda b, i: (b, i, 0)),
        out_shape=jax.ShapeDtypeStruct((B, S, D), F32),
        scratch_shapes=[pltpu.VMEM((tm, D), F32), pltpu.VMEM((tm, D), F32), pltpu.SemaphoreType.DMA(())],
        compiler_params=_params(("arbitrary", "arbitrary")),
        name="moe_combine",
    )(dest, x, gcol, gate, final_g.reshape(1, D), y_buf)


def moe_layout(cnt, idx, n_tokens, blk):
    counts = cnt[:N_EXPERTS, 0].astype(I32)
    padded = (counts + blk - 1) // blk * blk
    pad_end = jnp.cumsum(padded)
    pad_start = (pad_end - padded).astype(I32)
    n_blocks = (2 * n_tokens) // blk + N_EXPERTS
    n_valid = (pad_end[-1] // blk).astype(I32)
    starts = jnp.minimum(jnp.arange(n_blocks, dtype=I32), n_valid - 1) * blk
    block_e = jnp.minimum(jnp.sum(pad_end[None, :] <= starts[:, None], axis=1), N_EXPERTS - 1).astype(I32)
    e, pos = idx[:, 0:2, :], idx[:, 2:4, :]
    group_start = jnp.sum(jnp.where(e[..., None] == jnp.arange(N_EXPERTS, dtype=I32), pad_start, 0), axis=-1)
    dest = (pos + group_start).astype(I32)
    return dest, block_e, n_valid.reshape(1), n_blocks * blk


def rope_tables(n_tokens):
    t = jnp.arange(n_tokens)
    row = (t // GRID_W).astype(F32)
    col = (t % GRID_W).astype(F32)
    half = HEAD_DIM // 2
    freqs = ROPE_THETA ** (-jnp.arange(0, half, 2, dtype=F32) / half)
    ang = jnp.concatenate([row[:, None] * freqs, col[:, None] * freqs], axis=-1)
    cos = jnp.repeat(jnp.cos(ang), 2, axis=-1)
    sin = jnp.repeat(jnp.sin(ang), 2, axis=-1) * jnp.tile(jnp.array([-1.0, 1.0], F32), half)
    return cos, sin


def kernel(x, c, ctx, c_ctx, ada_w, ada_b, norm1_g, norm2_g, attn_w_in, attn_w_out, na_rpb, gqa_q_norm_g,
           gqa_k_norm_g, ffn_w_gate, ffn_w_up, ffn_w_down, conv_w_in, conv_w, conv_w_out, moe_w_router,
           moe_w_gate, moe_w_up, moe_w_down, final_norm_g):
    B, S, D = x.shape
    assert ada_w.shape[0] == 2 and S % GRID_W == 0 and S // GRID_W >= WIN_H
    kh = WIN_H

    n_cond = -(-(B + 1) // SUBLANES) * SUBLANES
    cond = jnp.zeros((n_cond, D), F32).at[:B].set(c).at[B].set(c_ctx)
    mod = adaln_terms(cond, ada_w, ada_b)

    def terms(layer, rows):
        m = mod[layer, rows].reshape(-1, 6, D)
        return [m[:, k][:, None, :] for k in range(6)]

    sh1, sc1, g1, sh2, sc2, g2 = terms(0, slice(0, B))
    csh1, csc1 = [jnp.broadcast_to(t, (B, 1, D)) for t in terms(0, slice(B, B + 1))[:2]]

    E, _, FF = moe_w_gate[0].shape
    moe_f32 = [moe_w_down[0].reshape(E * FF, D), moe_w_gate[0].reshape(E * D, FF), moe_w_up[0].reshape(E * D, FF)]

    def with_side_cast(fn, grid, arrays):
        if SideCast.fits(arrays, grid):
            out, *conv = fn(side_cast=arrays)
            return out, conv
        return fn(), [a.astype(BF16) for a in arrays]

    w_in = attn_w_in[0].astype(BF16)
    proj, (moe_wd,) = with_side_cast(
        functools.partial(norm_mod_matmul, x, norm1_g[0], sh1, sc1, w_in),
        (B, S // _tile(S, 1024), (Q_COLS + KV_COLS) // 512), moe_f32[:1])
    kvc = norm_mod_matmul(ctx, norm1_g[0], csh1, csc1, w_in, col_off=Q_COLS, tm=256)
    cos, sin = rope_tables(S)
    q_scale = HEAD_DIM ** -0.5 * LOG2_E
    qn = head_norm(proj, NA_COLS // GQA_Q_COLS, GQA_Q_HEADS, gqa_q_norm_g[0], cos, sin, rope=True,
                   out_scale=q_scale)
    kn = head_norm(proj, (Q_COLS + 2 * NA_COLS) // GQA_KV_COLS, GQA_KV_HEADS, gqa_k_norm_g[0], cos, sin, rope=True)
    L = ctx.shape[1]
    kcn = head_norm(kvc, 2 * NA_COLS // GQA_KV_COLS, GQA_KV_HEADS, gqa_k_norm_g[0], cos[:L], sin[:L], rope=False,
                    tm=L)
    out_a = neighborhood_attention(proj, kvc, na_bias_table(na_rpb[0], kh), kh)
    score_bound = (HEAD_DIM * q_scale * BF16_ROUNDING_MARGIN
                   * jnp.max(jnp.abs(gqa_q_norm_g[0])) * jnp.max(jnp.abs(gqa_k_norm_g[0])))
    out_b = lax.cond(
        score_bound <= GQA_UNSHIFTED_SCORE_LIMIT,
        functools.partial(gqa_attention, subtract_max=False),
        functools.partial(gqa_attention, subtract_max=True),
        qn, kn, proj, kcn, kvc)
    x = matmul_residual([out_a, out_b], attn_w_out[0].astype(BF16), x, g1)

    x, (moe_wg, moe_wu) = with_side_cast(
        functools.partial(dense_ffn, x, norm2_g[0], sh2, sc2, g2, ffn_w_gate[0].astype(BF16),
                          ffn_w_up[0].astype(BF16), ffn_w_down[0].astype(BF16)),
        dense_ffn_grid(x.shape, ffn_w_gate.shape[2]), moe_f32[1:])

    sh1, sc1, g1, sh2, sc2, g2 = terms(1, slice(0, B))
    z = conv_mixer_inner(x, norm1_g[1], sh1, sc1, conv_w_in[0].astype(BF16), conv_w[0])
    x = matmul_residual([z], conv_w_out[0].astype(BF16), x, g1)

    moe_blk = 1024 if (2 * B * S) % 1024 == 0 else 2 * B * S // N_EXPERTS
    hp, idx, gcol, cnt = moe_router(x, norm2_g[1], sh2, sc2, moe_w_router[0])
    dest, block_e, n_valid, n_slots = moe_layout(cnt, idx, B * S, moe_blk)
    xs = moe_dispatch(dest, hp, n_slots)
    y_buf = moe_expert_ffn(block_e, n_valid, xs, moe_wg.reshape(E, D, FF), moe_wu.reshape(E, D, FF),
                           moe_wd.reshape(E, FF, D), tm=moe_blk)
    return moe_combine(dest, x, gcol, g2, final_norm_g, y_buf)
```

```python
import functools

import numpy as np
import jax
import jax.numpy as jnp
from jax import lax
from jax.experimental import pallas as pl
from jax.experimental.pallas import tpu as pltpu

F32 = jnp.float32
BF16 = jnp.bfloat16
I32 = jnp.int32
U32 = jnp.uint32

GRID_W = 64
HEAD_DIM = 128
NA_HEADS = 8
GQA_Q_HEADS = 8
GQA_KV_HEADS = 2
GQA_GROUP = GQA_Q_HEADS // GQA_KV_HEADS
WIN_H = 8
WIN_W = 16
ROPE_THETA = 10000.0
N_EXPERTS = 8
NORM_EPS = 1e-6
NA_COLS = NA_HEADS * HEAD_DIM
GQA_Q_COLS = GQA_Q_HEADS * HEAD_DIM
GQA_KV_COLS = GQA_KV_HEADS * HEAD_DIM
Q_COLS = NA_COLS + GQA_Q_COLS
KV_COLS = 2 * NA_COLS + 2 * GQA_KV_COLS

LANES = 128
SUBLANES = 8
V7X_VMEM_LIMIT = 56 * 1024 * 1024
MASK_VALUE = -1e30
NT_DIMS = (((1,), (1,)), ((), ()))
LOG2_E = 1.4426950408889634
GQA_UNSHIFTED_SCORE_LIMIT = 64.0
BF16_ROUNDING_MARGIN = 1.02


def _tile(dim, pref):
    return pref if dim % pref == 0 else dim


def _params(sem, vmem=V7X_VMEM_LIMIT):
    return pltpu.CompilerParams(dimension_semantics=sem, vmem_limit_bytes=vmem)


def _norm_mod(x, g, shift, scale):
    y = x * lax.rsqrt(jnp.mean(x * x, axis=-1, keepdims=True) + NORM_EPS)
    return y * (g * (1.0 + scale)) + shift


def _adaln_kernel(cond_ref, w_ref, b_ref, o_ref):
    a = jax.nn.silu(cond_ref[...])
    o_ref[0] = jnp.dot(a, w_ref[0], preferred_element_type=F32,
                       precision=lax.Precision.HIGHEST) + b_ref[0]


def adaln_terms(cond, ada_w, ada_b):
    L, D, N = ada_w.shape
    R = cond.shape[0]
    tn = _tile(N, 1024)
    return pl.pallas_call(
        _adaln_kernel,
        grid=(L, N // tn),
        in_specs=[
            pl.BlockSpec((R, D), lambda l, j: (0, 0)),
            pl.BlockSpec((1, D, tn), lambda l, j: (l, 0, j)),
            pl.BlockSpec((1, 1, tn), lambda l, j: (l, 0, j)),
        ],
        out_specs=pl.BlockSpec((1, R, tn), lambda l, j: (l, 0, j)),
        out_shape=jax.ShapeDtypeStruct((L, R, N), F32),
        compiler_params=_params(("parallel", "parallel")),
        name="adaln",
    )(cond, ada_w, ada_b.reshape(L, 1, N))


BF16_SUBLANES = 16


class SideCast:
    def __init__(self, arrays, grid):
        self.arrays = list(arrays)
        self.grid = grid
        B, nI, nJ = grid
        self.plans = []
        for a in self.arrays:
            R = a.shape[0]
            fits = [nj for nj in range(nJ, 0, -1)
                    if R % (B * nI * nj) == 0 and (R // (B * nI * nj)) % BF16_SUBLANES == 0]
            assert fits, "no even split of the rows over the grid"
            self.plans.append((fits[0], R // (B * nI * fits[0])))

    @staticmethod
    def fits(arrays, grid):
        try:
            SideCast(arrays, grid)
            return True
        except AssertionError:
            return False

    def specs(self):
        _, nI, _ = self.grid

        def spec(a, nj, rows):
            return pl.BlockSpec((rows, a.shape[1]), lambda b, i, j: ((b * nI + i) * nj + jnp.minimum(j, nj - 1), 0))

        return [spec(a, nj, rows) for a, (nj, rows) in zip(self.arrays, self.plans)]

    def out_shapes(self):
        return [jax.ShapeDtypeStruct(a.shape, BF16) for a in self.arrays]

    def run(self, src_refs, dst_refs):
        for src, dst in zip(src_refs, dst_refs):
            dst[...] = src[...].astype(BF16)


class NormAhead:
    def __init__(self, grid, tm):
        self.grid = grid
        _, _, nJ = grid
        self.chunks = next(c for c in (8, 4, 2, 1) if c <= nJ and tm % c == 0 and (tm // c) % BF16_SUBLANES == 0)
        self.rows = tm // self.chunks

    def next_tile(self, b, i):
        B, nI, _ = self.grid
        lin = jnp.minimum(b * nI + i + 1, B * nI - 1)
        return lin // nI, lin % nI

    def specs(self, tm, D):
        def vec(b, i, j):
            return (self.next_tile(b, i)[0], 0, 0)

        return [pl.BlockSpec((1, tm, D), lambda b, i, j: (*self.next_tile(b, i), 0)),
                pl.BlockSpec((1, 1, D), vec), pl.BlockSpec((1, 1, D), vec)]

    def run(self, step, h_refs, x_ref, g_ref, sh_ref, sc_ref, xn_ref, shn_ref, scn_ref):
        _, nI, _ = self.grid
        lin = pl.program_id(0) * nI + pl.program_id(1)
        j = pl.program_id(2)

        @pl.when((lin == 0) & (j == 0))
        def _():
            h_refs[0][...] = _norm_mod(x_ref[0], g_ref[...], sh_ref[0], sc_ref[0]).astype(BF16)

        start = pl.multiple_of(jnp.minimum(j, self.chunks - 1) * self.rows, self.rows)
        rows = pl.ds(start, self.rows)
        for slot in range(2):
            @pl.when(lin % 2 == slot)
            def _():
                h_refs[1 - slot][rows, :] = _norm_mod(
                    xn_ref[0, rows, :], g_ref[...], shn_ref[0], scn_ref[0]).astype(BF16)
                step(h_refs[slot][...])


def _nm_matmul_kernel(*refs, side, ahead):
    n = len(side.arrays) if side else 0
    x_ref, g_ref, sh_ref, sc_ref, xn_ref, shn_ref, scn_ref, w_ref = refs[:8]
    o_ref = refs[8 + n]

    def step(h):
        o_ref[0] = jnp.dot(h, w_ref[...], preferred_element_type=F32).astype(o_ref.dtype)
        if side:
            side.run(refs[8:8 + n], refs[9 + n:9 + 2 * n])

    ahead.run(step, refs[-2:], x_ref, g_ref, sh_ref, sc_ref, xn_ref, shn_ref, scn_ref)


def norm_mod_matmul(x, g, shift, scale, w, *, col_off=0, n_cols=None, tm=1024, tn=512, side_cast=()):
    B, S, D = x.shape
    n_cols = w.shape[1] - col_off if n_cols is None else n_cols
    tm = _tile(S, tm)
    tn = _tile(n_cols, tn)
    assert col_off % tn == 0
    joff = col_off // tn
    grid = (B, S // tm, n_cols // tn)
    side = SideCast(side_cast, grid) if side_cast else None
    ahead = NormAhead(grid, tm)
    out = pl.pallas_call(
        functools.partial(_nm_matmul_kernel, side=side, ahead=ahead),
        grid=grid,
        in_specs=[
            pl.BlockSpec((1, tm, D), lambda b, i, j: (b, i, 0)),
            pl.BlockSpec((1, D), lambda b, i, j: (0, 0)),
            pl.BlockSpec((1, 1, D), lambda b, i, j: (b, 0, 0)),
            pl.BlockSpec((1, 1, D), lambda b, i, j: (b, 0, 0)),
        ] + ahead.specs(tm, D) + [
            pl.BlockSpec((D, tn), lambda b, i, j: (0, j + joff)),
        ] + (side.specs() if side else []),
        out_specs=[pl.BlockSpec((1, tm, tn), lambda b, i, j: (b, i, j))] + (side.specs() if side else []),
        out_shape=[jax.ShapeDtypeStruct((B, S, n_cols), BF16)] + (side.out_shapes() if side else []),
        scratch_shapes=[pltpu.VMEM((tm, D), BF16), pltpu.VMEM((tm, D), BF16)],
        compiler_params=_params(("arbitrary", "arbitrary", "arbitrary")),
        name="norm_mod_matmul",
    )(x, g.reshape(1, D), shift, scale, x, shift, scale, w, *side_cast)
    return out if side else out[0]


def _head_norm_kernel(x_ref, g_ref, cos_ref, sin_ref, o_ref, *, n_heads, rope, out_scale):
    g = g_ref[...]
    for h in range(n_heads):
        sl = slice(h * HEAD_DIM, (h + 1) * HEAD_DIM)
        xh = x_ref[0, :, sl].astype(F32)
        y = xh * lax.rsqrt(jnp.mean(xh * xh, axis=-1, keepdims=True) + NORM_EPS) * g
        if rope:
            lane = lax.broadcasted_iota(I32, y.shape, 1)
            swap = jnp.where((lane & 1) == 0, pltpu.roll(y, HEAD_DIM - 1, 1), pltpu.roll(y, 1, 1))
            y = y * cos_ref[...] + swap * sin_ref[...]
        if out_scale != 1.0:
            y = y * out_scale
        o_ref[0, :, sl] = y.astype(o_ref.dtype)


def head_norm(src, col_block, n_heads, g, cos, sin, *, rope, out_scale=1.0, tm=512):
    B, S, _ = src.shape
    w = n_heads * HEAD_DIM
    tm = _tile(S, tm)
    kern = functools.partial(_head_norm_kernel, n_heads=n_heads, rope=rope, out_scale=out_scale)
    return pl.pallas_call(
        kern,
        grid=(B, S // tm),
        in_specs=[
            pl.BlockSpec((1, tm, w), lambda b, i: (b, i, col_block)),
            pl.BlockSpec((1, HEAD_DIM), lambda b, i: (0, 0)),
            pl.BlockSpec((tm, HEAD_DIM), lambda b, i: (i, 0)),
            pl.BlockSpec((tm, HEAD_DIM), lambda b, i: (i, 0)),
        ],
        out_specs=pl.BlockSpec((1, tm, w), lambda b, i: (b, i, 0)),
        out_shape=jax.ShapeDtypeStruct((B, S, w), BF16),
        compiler_params=_params(("parallel", "parallel")),
        name="head_norm",
    )(src, g.reshape(1, HEAD_DIM), cos, sin)


NA_GROUP = 4
NA_GROUP_UNROLL = 4
NA_VARIANTS = 3


def _na_kernel(q_ref, k_ref, v_ref, kc_ref, vc_ref, bias_ref, o_ref, *, rows, kh, scale):
    kc = kc_ref[0]
    vc = vc_ref[0]
    n_groups = rows // NA_GROUP
    win = NA_GROUP + kh
    gq = NA_GROUP * GRID_W

    def body(g, carry):
        w0 = jnp.clip(g * NA_GROUP - kh // 2, 0, rows - win)
        variant = jnp.where(g == 0, 0, jnp.where(g == n_groups - 1, 2, 1))
        qs = pl.ds(pl.multiple_of(g * gq, gq), gq)
        band = pl.ds(pl.multiple_of(w0 * GRID_W, GRID_W), win * GRID_W)
        q = q_ref[0, qs, :]
        kb = k_ref[0, band, :]
        vb = v_ref[0, band, :]
        s_loc = lax.dot_general(q, kb, NT_DIMS, preferred_element_type=F32) * scale + bias_ref[0, variant]
        s_ctx = lax.dot_general(q, kc, NT_DIMS, preferred_element_type=F32) * scale
        m = jnp.maximum(jnp.max(s_loc, axis=-1, keepdims=True), jnp.max(s_ctx, axis=-1, keepdims=True))
        p_loc = jnp.exp2(s_loc - m)
        p_ctx = jnp.exp2(s_ctx - m)
        l = jnp.sum(p_loc, axis=-1, keepdims=True) + jnp.sum(p_ctx, axis=-1, keepdims=True)
        o = (jnp.dot(p_ctx.astype(BF16), vc, preferred_element_type=F32)
             + jnp.dot(p_loc.astype(BF16), vb, preferred_element_type=F32))
        o_ref[0, qs, :] = (o / l).astype(o_ref.dtype)
        return carry

    lax.fori_loop(0, n_groups, body, 0, unroll=NA_GROUP_UNROLL)


def _bias_table_kernel(rpb_ref, row_sel_ref, col_sel_ref, valid_ref, o_ref):
    rows = jnp.dot(row_sel_ref[...], rpb_ref[0], preferred_element_type=F32, precision=lax.Precision.HIGHEST)
    tab = jnp.dot(rows, col_sel_ref[...], preferred_element_type=F32, precision=lax.Precision.HIGHEST)
    o_ref[0] = jnp.where(valid_ref[...] > 0.0, tab * LOG2_E, MASK_VALUE)


def na_bias_table(rpb, kh):
    H, n_dr, n_dc = rpb.shape
    win = NA_GROUP + kh
    cols = np.arange(GRID_W)
    col_start = np.clip(cols - WIN_W // 2, 0, GRID_W - WIN_W)
    col_valid = (cols[None, :] >= col_start[:, None]) & (cols[None, :] < col_start[:, None] + WIN_W)
    dc_idx = np.clip(cols[None, :] - cols[:, None] + WIN_W - 1, 0, 2 * WIN_W - 2)
    ql = np.arange(NA_GROUP)[:, None]
    wr = np.arange(win)[None, :]
    band0 = [np.zeros_like(ql), ql, np.full_like(ql, win - kh)]
    dr = [wr - ql + WIN_H - 1, wr - ql + WIN_H - 1 - kh // 2, wr - ql + WIN_H - 1 - kh]
    pr = -(-(n_dr + 1) // SUBLANES) * SUBLANES
    pc = -(-n_dc // SUBLANES) * SUBLANES
    mask_row = pr - 1
    n_rows = NA_VARIANTS * NA_GROUP * win
    row_pick = np.concatenate([np.where((wr >= b0) & (wr < b0 + kh), d, mask_row).reshape(-1)
                               for b0, d in zip(band0, dr)])
    assert row_pick.min() >= 0 and (row_pick[row_pick != mask_row] < n_dr).all()
    row_sel = np.zeros((n_rows, pr), np.float32)
    row_sel[np.arange(n_rows), row_pick] = 1.0
    col_sel = np.zeros((pc, GRID_W * GRID_W), np.float32)
    col_sel[dc_idx.reshape(-1), np.arange(GRID_W * GRID_W)] = 1.0
    valid = col_valid.reshape(1, -1).astype(np.float32)
    rpb_p = jnp.zeros((H, pr, pc), F32).at[:, :n_dr, :n_dc].set(rpb.astype(F32)).at[:, mask_row, :].set(MASK_VALUE)
    tab = pl.pallas_call(
        _bias_table_kernel,
        grid=(H,),
        in_specs=[
            pl.BlockSpec((1, pr, pc), lambda h: (h, 0, 0)),
            pl.BlockSpec((n_rows, pr), lambda h: (0, 0)),
            pl.BlockSpec((pc, GRID_W * GRID_W), lambda h: (0, 0)),
            pl.BlockSpec((1, GRID_W * GRID_W), lambda h: (0, 0)),
        ],
        out_specs=pl.BlockSpec((1, n_rows, GRID_W * GRID_W), lambda h: (h, 0, 0)),
        out_shape=jax.ShapeDtypeStruct((H, n_rows, GRID_W * GRID_W), F32),
        compiler_params=_params(("parallel",)),
        name="na_bias_table",
    )(rpb_p, jnp.asarray(row_sel), jnp.asarray(col_sel), jnp.asarray(valid))
    tab = tab.reshape(H, NA_VARIANTS, NA_GROUP, win, GRID_W, GRID_W)
    return tab.transpose(0, 1, 2, 4, 3, 5).reshape(H, NA_VARIANTS, NA_GROUP * GRID_W, win * GRID_W)


def neighborhood_attention(proj, kvc, bias, kh):
    B, S, _ = proj.shape
    L = kvc.shape[1]
    rows = S // GRID_W
    assert rows % NA_GROUP == 0 and rows // NA_GROUP >= NA_VARIANTS
    qb, kb, vb = 0, Q_COLS // HEAD_DIM, (Q_COLS + NA_COLS) // HEAD_DIM
    kern = functools.partial(_na_kernel, rows=rows, kh=kh, scale=HEAD_DIM ** -0.5 * LOG2_E)
    return pl.pallas_call(
        kern,
        grid=(B, NA_HEADS),
        in_specs=[
            pl.BlockSpec((1, S, HEAD_DIM), lambda b, h: (b, 0, qb + h)),
            pl.BlockSpec((1, S, HEAD_DIM), lambda b, h: (b, 0, kb + h)),
            pl.BlockSpec((1, S, HEAD_DIM), lambda b, h: (b, 0, vb + h)),
            pl.BlockSpec((1, L, HEAD_DIM), lambda b, h: (b, 0, h)),
            pl.BlockSpec((1, L, HEAD_DIM), lambda b, h: (b, 0, NA_HEADS + h)),
            pl.BlockSpec((1,) + bias.shape[1:], lambda b, h: (h, 0, 0, 0)),
        ],
        out_specs=pl.BlockSpec((1, S, HEAD_DIM), lambda b, h: (b, 0, h)),
        out_shape=jax.ShapeDtypeStruct((B, S, NA_COLS), BF16),
        compiler_params=_params(("parallel", "parallel")),
        name="na_attention",
    )(proj, proj, proj, kvc, kvc, bias)


def _gqa_kernel(q_ref, k_ref, v_ref, kc_ref, vc_ref, o_ref, vx_ref, vcx_ref, *, subtract_max):
    @pl.when(pl.program_id(2) == 0)
    def _():
        vx_ref[:, :HEAD_DIM] = v_ref[0]
        vx_ref[:, HEAD_DIM:] = jnp.ones_like(v_ref[0])
        vcx_ref[:, :HEAD_DIM] = vc_ref[0]
        vcx_ref[:, HEAD_DIM:] = jnp.ones_like(vc_ref[0])

    k = k_ref[0]
    kc = kc_ref[0]
    vx = vx_ref[...]
    vcx = vcx_ref[...]
    for r in range(GQA_GROUP):
        sl = slice(r * HEAD_DIM, (r + 1) * HEAD_DIM)
        q = q_ref[0, :, sl]
        s_ctx = lax.dot_general(q, kc, NT_DIMS, preferred_element_type=F32)
        s_lat = lax.dot_general(q, k, NT_DIMS, preferred_element_type=F32)
        if subtract_max:
            m = jnp.maximum(jnp.max(s_ctx, axis=-1, keepdims=True), jnp.max(s_lat, axis=-1, keepdims=True))
            s_ctx = s_ctx - m
            s_lat = s_lat - m
        p_ctx = jnp.exp2(s_ctx).astype(BF16)
        p_lat = jnp.exp2(s_lat).astype(BF16)
        ol = (jnp.dot(p_ctx, vcx, preferred_element_type=F32) + jnp.dot(p_lat, vx, preferred_element_type=F32))
        o_ref[0, :, sl] = (ol[:, :HEAD_DIM] / ol[:, HEAD_DIM:]).astype(o_ref.dtype)


def gqa_attention(qn, kn, proj, kcn, kvc, *, subtract_max, tq=256):
    B, S, _ = qn.shape
    L = kcn.shape[1]
    tq = _tile(S, tq)
    gw = GQA_GROUP * HEAD_DIM
    vblk = (Q_COLS + 2 * NA_COLS + GQA_KV_COLS) // HEAD_DIM
    vcblk = (2 * NA_COLS + GQA_KV_COLS) // HEAD_DIM
    return pl.pallas_call(
        functools.partial(_gqa_kernel, subtract_max=subtract_max),
        grid=(B, GQA_KV_HEADS, S // tq),
        in_specs=[
            pl.BlockSpec((1, tq, gw), lambda b, g, i: (b, i, g)),
            pl.BlockSpec((1, S, HEAD_DIM), lambda b, g, i: (b, 0, g)),
            pl.BlockSpec((1, S, HEAD_DIM), lambda b, g, i: (b, 0, vblk + g)),
            pl.BlockSpec((1, L, HEAD_DIM), lambda b, g, i: (b, 0, g)),
            pl.BlockSpec((1, L, HEAD_DIM), lambda b, g, i: (b, 0, vcblk + g)),
        ],
        out_specs=pl.BlockSpec((1, tq, gw), lambda b, g, i: (b, i, g)),
        out_shape=jax.ShapeDtypeStruct((B, S, GQA_Q_COLS), BF16),
        scratch_shapes=[pltpu.VMEM((S, 2 * HEAD_DIM), BF16), pltpu.VMEM((L, 2 * HEAD_DIM), BF16)],
        compiler_params=_params(("parallel", "parallel", "arbitrary")),
        name="gqa_attention" if subtract_max else "gqa_attention_bounded",
    )(qn, kn, proj, kcn, kvc)


def _mm_res_kernel(*refs, k_sizes):
    n = len(k_sizes)
    w_ref, x_ref, gate_ref, o_ref = refs[n:]
    acc = None
    off = 0
    for a_ref, ks in zip(refs[:n], k_sizes):
        part = jnp.dot(a_ref[0], w_ref[off:off + ks, :], preferred_element_type=F32)
        acc = part if acc is None else acc + part
        off += ks
    o_ref[0] = x_ref[0] + gate_ref[0] * acc


def matmul_residual(lhs_list, w, x, gate, *, tm=512, tn=2048):
    B, S, N = x.shape
    K = w.shape[0]
    k_sizes = tuple(a.shape[-1] for a in lhs_list)
    assert sum(k_sizes) == K
    tm = _tile(S, tm)
    tn = _tile(N, tn)
    lhs_specs = [pl.BlockSpec((1, tm, ks), lambda b, i, j: (b, i, 0)) for ks in k_sizes]
    return pl.pallas_call(
        functools.partial(_mm_res_kernel, k_sizes=k_sizes),
        grid=(B, S // tm, N // tn),
        in_specs=lhs_specs + [
            pl.BlockSpec((K, tn), lambda b, i, j: (0, j)),
            pl.BlockSpec((1, tm, tn), lambda b, i, j: (b, i, j)),
            pl.BlockSpec((1, 1, tn), lambda b, i, j: (b, 0, j)),
        ],
        out_specs=pl.BlockSpec((1, tm, tn), lambda b, i, j: (b, i, j)),
        out_shape=jax.ShapeDtypeStruct((B, S, N), F32),
        compiler_params=_params(("parallel", "parallel", "arbitrary")),
        name="matmul_residual",
    )(*lhs_list, w, x, gate)


DOWN_CHUNK = 512


def _accumulate_down(acc_ref, a, wd_ref):
    n = acc_ref.shape[1]
    step = DOWN_CHUNK if n % DOWN_CHUNK == 0 else n
    for n0 in range(0, n, step):
        acc_ref[:, n0:n0 + step] += jnp.dot(a, wd_ref[:, n0:n0 + step], preferred_element_type=F32)


def _ffn_kernel(*refs, side, ahead):
    n = len(side.arrays) if side else 0
    x_ref, g_ref, sh_ref, sc_ref, xn_ref, shn_ref, scn_ref, gate_ref, wg_ref, wu_ref, wd_ref = refs[:11]
    o_ref = refs[11 + n]
    h0_ref, h1_ref, acc_ref = refs[-3:]
    j = pl.program_id(2)

    @pl.when(j == 0)
    def _():
        acc_ref[...] = jnp.zeros_like(acc_ref)

    def step(h):
        a = (jax.nn.silu(jnp.dot(h, wg_ref[...], preferred_element_type=F32))
             * jnp.dot(h, wu_ref[...], preferred_element_type=F32)).astype(BF16)
        _accumulate_down(acc_ref, a, wd_ref)
        if side:
            side.run(refs[11:11 + n], refs[12 + n:12 + 2 * n])

    ahead.run(step, (h0_ref, h1_ref), x_ref, g_ref, sh_ref, sc_ref, xn_ref, shn_ref, scn_ref)

    @pl.when(j == pl.num_programs(2) - 1)
    def _():
        o_ref[0] = x_ref[0] + gate_ref[0] * acc_ref[...]


def dense_ffn_grid(x_shape, ff, tm=512, tf=512):
    B, S, _ = x_shape
    return (B, S // _tile(S, tm), ff // _tile(ff, tf))


def dense_ffn(x, g, shift, scale, gate, wg, wu, wd, *, tm=512, tf=512, side_cast=()):
    B, S, D = x.shape
    FF = wg.shape[1]
    tm = _tile(S, tm)
    tf = _tile(FF, tf)
    grid = dense_ffn_grid(x.shape, FF, tm, tf)
    side = SideCast(side_cast, grid) if side_cast else None
    ahead = NormAhead(grid, tm)
    vec = pl.BlockSpec((1, 1, D), lambda b, i, j: (b, 0, 0))
    out = pl.pallas_call(
        functools.partial(_ffn_kernel, side=side, ahead=ahead),
        grid=grid,
        in_specs=[
            pl.BlockSpec((1, tm, D), lambda b, i, j: (b, i, 0)),
            pl.BlockSpec((1, D), lambda b, i, j: (0, 0)),
            vec, vec,
        ] + ahead.specs(tm, D) + [
            vec,
            pl.BlockSpec((D, tf), lambda b, i, j: (0, j)),
            pl.BlockSpec((D, tf), lambda b, i, j: (0, j)),
            pl.BlockSpec((tf, D), lambda b, i, j: (j, 0)),
        ] + (side.specs() if side else []),
        out_specs=[pl.BlockSpec((1, tm, D), lambda b, i, j: (b, i, 0))] + (side.specs() if side else []),
        out_shape=[jax.ShapeDtypeStruct((B, S, D), F32)] + (side.out_shapes() if side else []),
        scratch_shapes=[pltpu.VMEM((tm, D), BF16), pltpu.VMEM((tm, D), BF16), pltpu.VMEM((tm, D), F32)],
        compiler_params=_params(("arbitrary", "arbitrary", "arbitrary")),
        name="dense_ffn",
    )(x, g.reshape(1, D), shift, scale, x, shift, scale, gate, wg, wu, wd, *side_cast)
    return out if side else out[0]


def _conv_kernel(x_ref, xp_ref, xn_ref, g_ref, sh_ref, sc_ref, wb_ref, wc_ref, wx_ref, cw_ref,
                 o_ref, h_ref):
    i = pl.program_id(1)
    tm = x_ref.shape[1]

    @pl.when(pl.program_id(2) == 0)
    def _():
        g, sh, sc = g_ref[...], sh_ref[0], sc_ref[0]
        h_ref[:tm, :] = _norm_mod(x_ref[0], g, sh, sc).astype(BF16)
        halo = jnp.concatenate([xp_ref[0], xn_ref[0]], axis=0)
        h_ref[tm:, :] = _norm_mod(halo, g, sh, sc).astype(BF16)

    h = h_ref[...]
    bg = jnp.dot(h_ref[:tm, :], wb_ref[...], preferred_element_type=F32)
    ux = (jnp.dot(h, wc_ref[...], preferred_element_type=F32)
          * jnp.dot(h, wx_ref[...], preferred_element_type=F32))
    u = ux[:tm]
    uh = ux[tm:]
    prev = jnp.where(i > 0, uh[SUBLANES - 1:SUBLANES, :], 0.0)
    nxt = jnp.where(i < pl.num_programs(1) - 1, uh[SUBLANES:SUBLANES + 1, :], 0.0)
    row = lax.broadcasted_iota(I32, u.shape, 0)
    up = jnp.where(row == 0, prev, pltpu.roll(u, 1, 0))
    dn = jnp.where(row == tm - 1, nxt, pltpu.roll(u, tm - 1, 0))
    cw = cw_ref[...]
    y = cw[0:1, :] * up + cw[1:2, :] * u + cw[2:3, :] * dn
    o_ref[0] = (bg * y).astype(o_ref.dtype)


def conv_mixer_inner(x, g, shift, scale, w_in, conv_w, *, tm=1024, tc=512):
    B, S, D = x.shape
    tm = _tile(S, tm)
    tc = _tile(D, tc)
    nc = D // tc
    hb = tm // SUBLANES
    last_hb = S // SUBLANES - 1
    vec = pl.BlockSpec((1, 1, D), lambda b, i, j: (b, 0, 0))
    return pl.pallas_call(
        _conv_kernel,
        grid=(B, S // tm, nc),
        in_specs=[
            pl.BlockSpec((1, tm, D), lambda b, i, j: (b, i, 0)),
            pl.BlockSpec((1, SUBLANES, D), lambda b, i, j: (b, jnp.maximum(i * hb - 1, 0), 0)),
            pl.BlockSpec((1, SUBLANES, D), lambda b, i, j: (b, jnp.minimum((i + 1) * hb, last_hb), 0)),
            pl.BlockSpec((1, D), lambda b, i, j: (0, 0)),
            vec, vec,
            pl.BlockSpec((D, tc), lambda b, i, j: (0, j)),
            pl.BlockSpec((D, tc), lambda b, i, j: (0, nc + j)),
            pl.BlockSpec((D, tc), lambda b, i, j: (0, 2 * nc + j)),
            pl.BlockSpec((3, tc), lambda b, i, j: (0, j)),
        ],
        out_specs=pl.BlockSpec((1, tm, tc), lambda b, i, j: (b, i, j)),
        out_shape=jax.ShapeDtypeStruct((B, S, D), BF16),
        scratch_shapes=[pltpu.VMEM((tm + 2 * SUBLANES, D), BF16)],
        compiler_params=_params(("parallel", "parallel", "arbitrary")),
        name="conv_mixer",
    )(x, x, x, g.reshape(1, D), shift, scale, w_in, w_in, w_in, conv_w)


def _router_kernel(x_ref, g_ref, sh_ref, sc_ref, wr_ref, hp_ref, idx_ref, gcol_ref, cnt_ref, run_ref):
    tm, D = x_ref.shape[1], x_ref.shape[2]
    d2 = D // 2

    @pl.when((pl.program_id(0) == 0) & (pl.program_id(1) == 0))
    def _():
        run_ref[...] = jnp.zeros_like(run_ref)

    h = _norm_mod(x_ref[0], g_ref[...], sh_ref[0], sc_ref[0])

    lo = pltpu.bitcast(h[:, :d2].astype(BF16).astype(F32), U32)
    hi = pltpu.bitcast(h[:, d2:].astype(BF16).astype(F32), U32)
    hp_ref[0] = (hi & jnp.uint32(0xFFFF0000)) | (lo >> 16)

    logits = jnp.dot(h, wr_ref[...], preferred_element_type=F32, precision=lax.Precision.HIGHEST)
    lane = lax.broadcasted_iota(I32, logits.shape, 1).astype(F32)
    neg = jnp.float32(-jnp.inf)
    l1 = jnp.where(lane < N_EXPERTS, logits, neg)
    m1 = jnp.max(l1, axis=-1, keepdims=True)
    i1 = jnp.min(jnp.where(l1 == m1, lane, float(LANES)), axis=-1, keepdims=True)
    l2 = jnp.where(lane == i1, neg, l1)
    m2 = jnp.max(l2, axis=-1, keepdims=True)
    i2 = jnp.min(jnp.where(l2 == m2, lane, float(LANES)), axis=-1, keepdims=True)
    e = jnp.exp(m2 - m1)
    den = 1.0 + e
    gcol_ref[...] = jnp.where(lane == 0, 1.0 / den, jnp.where(lane == 1, e / den, 0.0))

    chosen = jnp.where((lane == i1) | (lane == i2), 1.0, 0.0)
    chosen_t = chosen.T
    rr = lax.broadcasted_iota(I32, (tm, tm), 0)
    cc = lax.broadcasted_iota(I32, (tm, tm), 1)
    before = jnp.where(rr < cc, 1.0, 0.0).astype(BF16)
    cum = jnp.dot(chosen_t.astype(BF16), before, preferred_element_type=F32) + run_ref[...]
    sel = jnp.where(lane == 0, i1, jnp.where(lane == 1, i2, 0.0)).T
    i1r = sel[0:1, :]
    i2r = sel[1:2, :]
    sub = lax.broadcasted_iota(I32, cum.shape, 0).astype(F32)
    p1r = jnp.sum(jnp.where(sub == i1r, cum, 0.0), axis=0, keepdims=True)
    p2r = jnp.sum(jnp.where(sub == i2r, cum, 0.0), axis=0, keepdims=True)
    idx_ref[0] = jnp.concatenate([i1r, i2r, p1r, p2r], axis=0).astype(I32)

    run_ref[...] += jnp.sum(chosen_t, axis=1, keepdims=True)
    cnt_ref[...] = jnp.broadcast_to(run_ref[...], cnt_ref.shape)


def moe_router(x, g, shift, scale, w_router, *, tm=512):
    B, S, D = x.shape
    E = w_router.shape[1]
    tm = _tile(S, tm)
    nS = S // tm
    wr = jnp.zeros((D, LANES), F32).at[:, :E].set(w_router.astype(F32))
    vec = pl.BlockSpec((1, 1, D), lambda b, i: (b, 0, 0))
    hp, idx, gcol, cnt = pl.pallas_call(
        _router_kernel,
        grid=(B, nS),
        in_specs=[
            pl.BlockSpec((1, tm, D), lambda b, i: (b, i, 0)),
            pl.BlockSpec((1, D), lambda b, i: (0, 0)),
            vec, vec,
            pl.BlockSpec((D, LANES), lambda b, i: (0, 0)),
        ],
        out_specs=[
            pl.BlockSpec((1, tm, D // 2), lambda b, i: (b, i, 0)),
            pl.BlockSpec((1, 4, tm), lambda b, i: (b * nS + i, 0, 0)),
            pl.BlockSpec((tm, LANES), lambda b, i: (b * nS + i, 0)),
            pl.BlockSpec((LANES, LANES), lambda b, i: (0, 0)),
        ],
        out_shape=[
            jax.ShapeDtypeStruct((B, S, D // 2), U32),
            jax.ShapeDtypeStruct((B * nS, 4, tm), I32),
            jax.ShapeDtypeStruct((B * S, LANES), F32),
            jax.ShapeDtypeStruct((LANES, LANES), F32),
        ],
        scratch_shapes=[pltpu.VMEM((LANES, 1), F32)],
        compiler_params=_params(("arbitrary", "arbitrary")),
        name="moe_router",
    )(x, g.reshape(1, D), shift, scale, wr)
    return hp.reshape(B * S, D // 2), idx, gcol, cnt


def _row_copy(src, dst, sem):
    return pltpu.make_async_copy(src, dst, sem)


ROW_DMA_UNROLL = 8
ROW_WAIT_UNROLL = 64


def _wait_rows(n, src_row, dst_row, sem):
    def wait(i, c):
        for _ in range(ROW_WAIT_UNROLL):
            _row_copy(src_row, dst_row, sem).wait()
        return c

    assert n % ROW_WAIT_UNROLL == 0
    lax.fori_loop(0, n // ROW_WAIT_UNROLL, wait, 0)


def _dispatch_kernel(dest_ref, hp_ref, xs_in_ref, xs_ref, sem):
    del xs_in_ref
    tm = hp_ref.shape[0]

    def start(t, c):
        for k in range(2):
            _row_copy(hp_ref.at[pl.ds(t, 1)], xs_ref.at[pl.ds(dest_ref[0, k, t], 1)], sem).start()
        return c

    lax.fori_loop(0, tm, start, 0, unroll=ROW_DMA_UNROLL)
    _wait_rows(2 * tm, hp_ref.at[pl.ds(0, 1)], xs_ref.at[pl.ds(0, 1)], sem)


def moe_dispatch(dest, hp, n_slots):
    N, d2 = hp.shape
    nT, _, tm = dest.shape
    xs0 = jnp.zeros((n_slots, d2), U32)
    return pl.pallas_call(
        _dispatch_kernel,
        grid=(nT,),
        in_specs=[
            pl.BlockSpec((1, 2, tm), lambda t: (t, 0, 0), memory_space=pltpu.SMEM),
            pl.BlockSpec((tm, d2), lambda t: (t, 0)),
            pl.BlockSpec(memory_space=pl.ANY),
        ],
        out_specs=pl.BlockSpec(memory_space=pl.ANY),
        out_shape=jax.ShapeDtypeStruct((n_slots, d2), U32),
        scratch_shapes=[pltpu.SemaphoreType.DMA(())],
        input_output_aliases={2: 0},
        compiler_params=_params(("arbitrary",)),
        name="moe_dispatch",
    )(dest, hp, xs0)


def _moe_ffn_kernel(be_ref, nv_ref, xs_ref, wg_ref, wu_ref, wd_ref, y_ref, xb_ref):
    b = pl.program_id(0)
    j = pl.program_id(1)
    d2 = xs_ref.shape[1]

    @pl.when((b >= nv_ref[0]) & (j == 0))
    def _():
        y_ref[...] = jnp.zeros_like(y_ref)

    @pl.when(b < nv_ref[0])
    def _():
        @pl.when(j == 0)
        def _():
            w = xs_ref[...]
            xb_ref[:, :d2] = pltpu.bitcast(w << 16, F32).astype(BF16)
            xb_ref[:, d2:] = pltpu.bitcast(w & jnp.uint32(0xFFFF0000), F32).astype(BF16)

            y_ref[...] = jnp.zeros_like(y_ref)

        x = xb_ref[...]
        a = (jax.nn.silu(jnp.dot(x, wg_ref[0], preferred_element_type=F32))
             * jnp.dot(x, wu_ref[0], preferred_element_type=F32)).astype(BF16)
        _accumulate_down(y_ref, a, wd_ref.at[0])


def moe_expert_ffn(block_e, n_valid, xs, wg, wu, wd, *, tm, tf=512):
    n_slots, d2 = xs.shape
    D = 2 * d2
    FF = wg.shape[2]
    tf = _tile(FF, tf)
    nj = FF // tf
    nb = n_slots // tm

    def row_map(b, j, be, nv):
        return (jnp.minimum(b, nv[0] - 1), 0)

    def col_of(b, j, nv):
        return jnp.where(b < nv[0], j, nj - 1)

    grid_spec = pltpu.PrefetchScalarGridSpec(
        num_scalar_prefetch=2,
        grid=(nb, nj),
        in_specs=[
            pl.BlockSpec((tm, d2), row_map),
            pl.BlockSpec((1, D, tf), lambda b, j, be, nv: (be[b], 0, col_of(b, j, nv))),
            pl.BlockSpec((1, D, tf), lambda b, j, be, nv: (be[b], 0, col_of(b, j, nv))),
            pl.BlockSpec((1, tf, D), lambda b, j, be, nv: (be[b], col_of(b, j, nv), 0)),
        ],
        out_specs=pl.BlockSpec((tm, D), lambda b, j, be, nv: (b, 0)),
        scratch_shapes=[pltpu.VMEM((tm, D), BF16)],
    )
    return pl.pallas_call(
        _moe_ffn_kernel,
        grid_spec=grid_spec,
        out_shape=jax.ShapeDtypeStruct((n_slots, D), F32),
        compiler_params=_params(("arbitrary", "arbitrary")),
        name="moe_expert_ffn",
    )(block_e, n_valid, xs, wg, wu, wd)


def _combine_kernel(dest_ref, x_ref, gcol_ref, gate_ref, fg_ref, y_ref, o_ref, ya_ref, yb_ref, sem):
    tm = x_ref.shape[1]
    bufs = (ya_ref, yb_ref)

    def start(t, c):
        for k in range(2):
            _row_copy(y_ref.at[pl.ds(dest_ref[0, k, t], 1)], bufs[k].at[pl.ds(t, 1)], sem).start()
        return c

    lax.fori_loop(0, tm, start, 0, unroll=ROW_DMA_UNROLL)
    _wait_rows(2 * tm, y_ref.at[pl.ds(0, 1)], ya_ref.at[pl.ds(0, 1)], sem)

    moe = gcol_ref[:, 0:1] * ya_ref[...] + gcol_ref[:, 1:2] * yb_ref[...]
    x = x_ref[0] + gate_ref[0] * moe
    y = x * lax.rsqrt(jnp.mean(x * x, axis=-1, keepdims=True) + NORM_EPS)
    o_ref[0] = y * fg_ref[...]


def moe_combine(dest, x, gcol, gate, final_g, y_buf):
    B, S, D = x.shape
    nT, _, tm = dest.shape
    nS = S // tm
    return pl.pallas_call(
        _combine_kernel,
        grid=(B, nS),
        in_specs=[
            pl.BlockSpec((1, 2, tm), lambda b, i: (b * nS + i, 0, 0), memory_space=pltpu.SMEM),
            pl.BlockSpec((1, tm, D), lambda b, i: (b, i, 0)),
            pl.BlockSpec((tm, LANES), lambda b, i: (b * nS + i, 0)),
            pl.BlockSpec((1, 1, D), lambda b, i: (b, 0, 0)),
            pl.BlockSpec((1, D), lambda b, i: (0, 0)),
            pl.BlockSpec(memory_space=pl.ANY),
        ],
        out_specs=pl.BlockSpec((1, tm, D), lambda b, i: (b, i, 0)),
        out_shape=jax.ShapeDtypeStruct((B, S, D), F32),
        scratch_shapes=[pltpu.VMEM((tm, D), F32), pltpu.VMEM((tm, D), F32), pltpu.SemaphoreType.DMA(())],
        compiler_params=_params(("arbitrary", "arbitrary")),
        name="moe_combine",
    )(dest, x, gcol, gate, final_g.reshape(1, D), y_buf)


def moe_layout(cnt, idx, n_tokens, blk):
    counts = cnt[:N_EXPERTS, 0].astype(I32)
    padded = (counts + blk - 1) // blk * blk
    pad_end = jnp.cumsum(padded)
    pad_start = (pad_end - padded).astype(I32)
    n_blocks = (2 * n_tokens) // blk + N_EXPERTS
    n_valid = (pad_end[-1] // blk).astype(I32)
    starts = jnp.minimum(jnp.arange(n_blocks, dtype=I32), n_valid - 1) * blk
    block_e = jnp.minimum(jnp.sum(pad_end[None, :] <= starts[:, None], axis=1), N_EXPERTS - 1).astype(I32)
    e, pos = idx[:, 0:2, :], idx[:, 2:4, :]
    group_start = jnp.sum(jnp.where(e[..., None] == jnp.arange(N_EXPERTS, dtype=I32), pad_start, 0), axis=-1)
    dest = (pos + group_start).astype(I32)
    return dest, block_e, n_valid.reshape(1), n_blocks * blk


def rope_tables(n_tokens):
    t = jnp.arange(n_tokens)
    row = (t // GRID_W).astype(F32)
    col = (t % GRID_W).astype(F32)
    half = HEAD_DIM // 2
    freqs = ROPE_THETA ** (-jnp.arange(0, half, 2, dtype=F32) / half)
    ang = jnp.concatenate([row[:, None] * freqs, col[:, None] * freqs], axis=-1)
    cos = jnp.repeat(jnp.cos(ang), 2, axis=-1)
    sin = jnp.repeat(jnp.sin(ang), 2, axis=-1) * jnp.tile(jnp.array([-1.0, 1.0], F32), half)
    return cos, sin


def kernel(x, c, ctx, c_ctx, ada_w, ada_b, norm1_g, norm2_g, attn_w_in, attn_w_out, na_rpb, gqa_q_norm_g,
           gqa_k_norm_g, ffn_w_gate, ffn_w_up, ffn_w_down, conv_w_in, conv_w, conv_w_out, moe_w_router,
           moe_w_gate, moe_w_up, moe_w_down, final_norm_g):
    B, S, D = x.shape
    assert ada_w.shape[0] == 2 and S % GRID_W == 0 and S // GRID_W >= WIN_H
    kh = WIN_H

    n_cond = -(-(B + 1) // SUBLANES) * SUBLANES
    cond = jnp.zeros((n_cond, D), F32).at[:B].set(c).at[B].set(c_ctx)
    mod = adaln_terms(cond, ada_w, ada_b)

    def terms(layer, rows):
        m = mod[layer, rows].reshape(-1, 6, D)
        return [m[:, k][:, None, :] for k in range(6)]

    sh1, sc1, g1, sh2, sc2, g2 = terms(0, slice(0, B))
    csh1, csc1 = [jnp.broadcast_to(t, (B, 1, D)) for t in terms(0, slice(B, B + 1))[:2]]

    E, _, FF = moe_w_gate[0].shape
    moe_f32 = [moe_w_down[0].reshape(E * FF, D), moe_w_gate[0].reshape(E * D, FF), moe_w_up[0].reshape(E * D, FF)]

    def with_side_cast(fn, grid, arrays):
        if SideCast.fits(arrays, grid):
            out, *conv = fn(side_cast=arrays)
            return out, conv
        return fn(), [a.astype(BF16) for a in arrays]

    w_in = attn_w_in[0].astype(BF16)
    proj, (moe_wd,) = with_side_cast(
        functools.partial(norm_mod_matmul, x, norm1_g[0], sh1, sc1, w_in),
        (B, S // _tile(S, 1024), (Q_COLS + KV_COLS) // 512), moe_f32[:1])
    kvc = norm_mod_matmul(ctx, norm1_g[0], csh1, csc1, w_in, col_off=Q_COLS, tm=256)
    cos, sin = rope_tables(S)
    q_scale = HEAD_DIM ** -0.5 * LOG2_E
    qn = head_norm(proj, NA_COLS // GQA_Q_COLS, GQA_Q_HEADS, gqa_q_norm_g[0], cos, sin, rope=True,
                   out_scale=q_scale)
    kn = head_norm(proj, (Q_COLS + 2 * NA_COLS) // GQA_KV_COLS, GQA_KV_HEADS, gqa_k_norm_g[0], cos, sin, rope=True)
    L = ctx.shape[1]
    kcn = head_norm(kvc, 2 * NA_COLS // GQA_KV_COLS, GQA_KV_HEADS, gqa_k_norm_g[0], cos[:L], sin[:L], rope=False,
                    tm=L)
    out_a = neighborhood_attention(proj, kvc, na_bias_table(na_rpb[0], kh), kh)
    score_bound = (HEAD_DIM * q_scale * BF16_ROUNDING_MARGIN
                   * jnp.max(jnp.abs(gqa_q_norm_g[0])) * jnp.max(jnp.abs(gqa_k_norm_g[0])))
    out_b = lax.cond(
        score_bound <= GQA_UNSHIFTED_SCORE_LIMIT,
        functools.partial(gqa_attention, subtract_max=False),
        functools.partial(gqa_attention, subtract_max=True),
        qn, kn, proj, kcn, kvc)
    x = matmul_residual([out_a, out_b], attn_w_out[0].astype(BF16), x, g1)

    x, (moe_wg, moe_wu) = with_side_cast(
        functools.partial(dense_ffn, x, norm2_g[0], sh2, sc2, g2, ffn_w_gate[0].astype(BF16),
                          ffn_w_up[0].astype(BF16), ffn_w_down[0].astype(BF16)),
        dense_ffn_grid(x.shape, ffn_w_gate.shape[2]), moe_f32[1:])

    sh1, sc1, g1, sh2, sc2, g2 = terms(1, slice(0, B))
    z = conv_mixer_inner(x, norm1_g[1], sh1, sc1, conv_w_in[0].astype(BF16), conv_w[0])
    x = matmul_residual([z], conv_w_out[0].astype(BF16), x, g1)

    moe_blk = 1024 if (2 * B * S) % 1024 == 0 else 2 * B * S // N_EXPERTS
    hp, idx, gcol, cnt = moe_router(x, norm2_g[1], sh2, sc2, moe_w_router[0])
    dest, block_e, n_valid, n_slots = moe_layout(cnt, idx, B * S, moe_blk)
    xs = moe_dispatch(dest, hp, n_slots)
    y_buf = moe_expert_ffn(block_e, n_valid, xs, moe_wg.reshape(E, D, FF), moe_wu.reshape(E, D, FF),
                           moe_wd.reshape(E, FF, D), tm=moe_blk)
    return moe_combine(dest, x, gcol, g2, final_norm_g, y_buf)
```

```python
import functools

import numpy as np
import jax
import jax.numpy as jnp
from jax import lax
from jax.experimental import pallas as pl
from jax.experimental.pallas import tpu as pltpu

F32 = jnp.float32
BF16 = jnp.bfloat16
I32 = jnp.int32
U32 = jnp.uint32

GRID_W = 64
HEAD_DIM = 128
NA_HEADS = 8
GQA_Q_HEADS = 8
GQA_KV_HEADS = 2
GQA_GROUP = GQA_Q_HEADS // GQA_KV_HEADS
WIN_H = 8
WIN_W = 16
ROPE_THETA = 10000.0
N_EXPERTS = 8
NORM_EPS = 1e-6
NA_COLS = NA_HEADS * HEAD_DIM
GQA_Q_COLS = GQA_Q_HEADS * HEAD_DIM
GQA_KV_COLS = GQA_KV_HEADS * HEAD_DIM
Q_COLS = NA_COLS + GQA_Q_COLS
KV_COLS = 2 * NA_COLS + 2 * GQA_KV_COLS

LANES = 128
SUBLANES = 8
V7X_VMEM_LIMIT = 56 * 1024 * 1024
MASK_VALUE = -1e30
NT_DIMS = (((1,), (1,)), ((), ()))
LOG2_E = 1.4426950408889634
GQA_UNSHIFTED_SCORE_LIMIT = 64.0
BF16_ROUNDING_MARGIN = 1.02


def _tile(dim, pref):
    return pref if dim % pref == 0 else dim


def _params(sem, vmem=V7X_VMEM_LIMIT):
    return pltpu.CompilerParams(dimension_semantics=sem, vmem_limit_bytes=vmem)


def _norm_mod(x, g, shift, scale):
    y = x * lax.rsqrt(jnp.mean(x * x, axis=-1, keepdims=True) + NORM_EPS)
    return y * (g * (1.0 + scale)) + shift


def _adaln_kernel(cond_ref, w_ref, b_ref, o_ref):
    a = jax.nn.silu(cond_ref[...])
    o_ref[0] = jnp.dot(a, w_ref[0], preferred_element_type=F32,
                       precision=lax.Precision.HIGHEST) + b_ref[0]


def adaln_terms(cond, ada_w, ada_b):
    L, D, N = ada_w.shape
    R = cond.shape[0]
    tn = _tile(N, 1024)
    return pl.pallas_call(
        _adaln_kernel,
        grid=(L, N // tn),
        in_specs=[
            pl.BlockSpec((R, D), lambda l, j: (0, 0)),
            pl.BlockSpec((1, D, tn), lambda l, j: (l, 0, j)),
            pl.BlockSpec((1, 1, tn), lambda l, j: (l, 0, j)),
        ],
        out_specs=pl.BlockSpec((1, R, tn), lambda l, j: (l, 0, j)),
        out_shape=jax.ShapeDtypeStruct((L, R, N), F32),
        compiler_params=_params(("parallel", "parallel")),
        name="adaln",
    )(cond, ada_w, ada_b.reshape(L, 1, N))


BF16_SUBLANES = 16


class SideCast:
    def __init__(self, arrays, grid):
        self.arrays = list(arrays)
        self.grid = grid
        B, nI, nJ = grid
        self.plans = []
        for a in self.arrays:
            R = a.shape[0]
            fits = [nj for nj in range(nJ, 0, -1)
                    if R % (B * nI * nj) == 0 and (R // (B * nI * nj)) % BF16_SUBLANES == 0]
            assert fits, "no even split of the rows over the grid"
            self.plans.append((fits[0], R // (B * nI * fits[0])))

    @staticmethod
    def fits(arrays, grid):
        try:
            SideCast(arrays, grid)
            return True
        except AssertionError:
            return False

    def specs(self):
        _, nI, _ = self.grid

        def spec(a, nj, rows):
            return pl.BlockSpec((rows, a.shape[1]), lambda b, i, j: ((b * nI + i) * nj + jnp.minimum(j, nj - 1), 0))

        return [spec(a, nj, rows) for a, (nj, rows) in zip(self.arrays, self.plans)]

    def out_shapes(self):
        return [jax.ShapeDtypeStruct(a.shape, BF16) for a in self.arrays]

    def run(self, src_refs, dst_refs):
        for src, dst in zip(src_refs, dst_refs):
            dst[...] = src[...].astype(BF16)


def _nm_matmul_kernel(x_ref, g_ref, sh_ref, sc_ref, w_ref, o_ref, h_ref):
    @pl.when(pl.program_id(2) == 0)
    def _():
        h_ref[...] = _norm_mod(x_ref[0], g_ref[...], sh_ref[0], sc_ref[0]).astype(BF16)

    o_ref[0] = jnp.dot(h_ref[...], w_ref[...], preferred_element_type=F32).astype(o_ref.dtype)


def norm_mod_matmul(x, g, shift, scale, w, *, col_off=0, n_cols=None, tm=1024, tn=512):
    B, S, D = x.shape
    n_cols = w.shape[1] - col_off if n_cols is None else n_cols
    tm = _tile(S, tm)
    tn = _tile(n_cols, tn)
    assert col_off % tn == 0
    joff = col_off // tn
    return pl.pallas_call(
        _nm_matmul_kernel,
        grid=(B, S // tm, n_cols // tn),
        in_specs=[
            pl.BlockSpec((1, tm, D), lambda b, i, j: (b, i, 0)),
            pl.BlockSpec((1, D), lambda b, i, j: (0, 0)),
            pl.BlockSpec((1, 1, D), lambda b, i, j: (b, 0, 0)),
            pl.BlockSpec((1, 1, D), lambda b, i, j: (b, 0, 0)),
            pl.BlockSpec((D, tn), lambda b, i, j: (0, j + joff)),
        ],
        out_specs=pl.BlockSpec((1, tm, tn), lambda b, i, j: (b, i, j)),
        out_shape=jax.ShapeDtypeStruct((B, S, n_cols), BF16),
        scratch_shapes=[pltpu.VMEM((tm, D), BF16)],
        compiler_params=_params(("parallel", "parallel", "arbitrary")),
        name="norm_mod_matmul",
    )(x, g.reshape(1, D), shift, scale, w)


def _head_norm_kernel(x_ref, g_ref, cos_ref, sin_ref, o_ref, *, n_heads, rope, out_scale):
    g = g_ref[...]
    for h in range(n_heads):
        sl = slice(h * HEAD_DIM, (h + 1) * HEAD_DIM)
        xh = x_ref[0, :, sl].astype(F32)
        y = xh * lax.rsqrt(jnp.mean(xh * xh, axis=-1, keepdims=True) + NORM_EPS) * g
        if rope:
            lane = lax.broadcasted_iota(I32, y.shape, 1)
            swap = jnp.where((lane & 1) == 0, pltpu.roll(y, HEAD_DIM - 1, 1), pltpu.roll(y, 1, 1))
            y = y * cos_ref[...] + swap * sin_ref[...]
        if out_scale != 1.0:
            y = y * out_scale
        o_ref[0, :, sl] = y.astype(o_ref.dtype)


def head_norm(src, col_block, n_heads, g, cos, sin, *, rope, out_scale=1.0, tm=512):
    B, S, _ = src.shape
    w = n_heads * HEAD_DIM
    tm = _tile(S, tm)
    kern = functools.partial(_head_norm_kernel, n_heads=n_heads, rope=rope, out_scale=out_scale)
    return pl.pallas_call(
        kern,
        grid=(B, S // tm),
        in_specs=[
            pl.BlockSpec((1, tm, w), lambda b, i: (b, i, col_block)),
            pl.BlockSpec((1, HEAD_DIM), lambda b, i: (0, 0)),
            pl.BlockSpec((tm, HEAD_DIM), lambda b, i: (i, 0)),
            pl.BlockSpec((tm, HEAD_DIM), lambda b, i: (i, 0)),
        ],
        out_specs=pl.BlockSpec((1, tm, w), lambda b, i: (b, i, 0)),
        out_shape=jax.ShapeDtypeStruct((B, S, w), BF16),
        compiler_params=_params(("parallel", "parallel")),
        name="head_norm",
    )(src, g.reshape(1, HEAD_DIM), cos, sin)


NA_GROUP = 4
NA_GROUP_UNROLL = 4
NA_VARIANTS = 3


def _na_kernel(q_ref, k_ref, v_ref, kc_ref, vc_ref, bias_ref, o_ref, *, rows, kh, scale):
    kc = kc_ref[0]
    vc = vc_ref[0]
    n_groups = rows // NA_GROUP
    win = NA_GROUP + kh
    gq = NA_GROUP * GRID_W

    def body(g, carry):
        w0 = jnp.clip(g * NA_GROUP - kh // 2, 0, rows - win)
        variant = jnp.where(g == 0, 0, jnp.where(g == n_groups - 1, 2, 1))
        qs = pl.ds(pl.multiple_of(g * gq, gq), gq)
        band = pl.ds(pl.multiple_of(w0 * GRID_W, GRID_W), win * GRID_W)
        q = q_ref[0, qs, :]
        kb = k_ref[0, band, :]
        vb = v_ref[0, band, :]
        s_loc = lax.dot_general(q, kb, NT_DIMS, preferred_element_type=F32) * scale + bias_ref[0, variant]
        s_ctx = lax.dot_general(q, kc, NT_DIMS, preferred_element_type=F32) * scale
        m = jnp.maximum(jnp.max(s_loc, axis=-1, keepdims=True), jnp.max(s_ctx, axis=-1, keepdims=True))
        p_loc = jnp.exp2(s_loc - m)
        p_ctx = jnp.exp2(s_ctx - m)
        l = jnp.sum(p_loc, axis=-1, keepdims=True) + jnp.sum(p_ctx, axis=-1, keepdims=True)
        o = (jnp.dot(p_ctx.astype(BF16), vc, preferred_element_type=F32)
             + jnp.dot(p_loc.astype(BF16), vb, preferred_element_type=F32))
        o_ref[0, qs, :] = (o / l).astype(o_ref.dtype)
        return carry

    lax.fori_loop(0, n_groups, body, 0, unroll=NA_GROUP_UNROLL)


def _bias_table_kernel(rpb_ref, row_sel_ref, col_sel_ref, valid_ref, o_ref):
    rows = jnp.dot(row_sel_ref[...], rpb_ref[0], preferred_element_type=F32, precision=lax.Precision.HIGHEST)
    tab = jnp.dot(rows, col_sel_ref[...], preferred_element_type=F32, precision=lax.Precision.HIGHEST)
    o_ref[0] = jnp.where(valid_ref[...] > 0.0, tab * LOG2_E, MASK_VALUE)


def na_bias_table(rpb, kh):
    H, n_dr, n_dc = rpb.shape
    win = NA_GROUP + kh
    cols = np.arange(GRID_W)
    col_start = np.clip(cols - WIN_W // 2, 0, GRID_W - WIN_W)
    col_valid = (cols[None, :] >= col_start[:, None]) & (cols[None, :] < col_start[:, None] + WIN_W)
    dc_idx = np.clip(cols[None, :] - cols[:, None] + WIN_W - 1, 0, 2 * WIN_W - 2)
    ql = np.arange(NA_GROUP)[:, None]
    wr = np.arange(win)[None, :]
    band0 = [np.zeros_like(ql), ql, np.full_like(ql, win - kh)]
    dr = [wr - ql + WIN_H - 1, wr - ql + WIN_H - 1 - kh // 2, wr - ql + WIN_H - 1 - kh]
    pr = -(-(n_dr + 1) // SUBLANES) * SUBLANES
    pc = -(-n_dc // SUBLANES) * SUBLANES
    mask_row = pr - 1
    n_rows = NA_VARIANTS * NA_GROUP * win
    row_pick = np.concatenate([np.where((wr >= b0) & (wr < b0 + kh), d, mask_row).reshape(-1)
                               for b0, d in zip(band0, dr)])
    assert row_pick.min() >= 0 and (row_pick[row_pick != mask_row] < n_dr).all()
    row_sel = np.zeros((n_rows, pr), np.float32)
    row_sel[np.arange(n_rows), row_pick] = 1.0
    col_sel = np.zeros((pc, GRID_W * GRID_W), np.float32)
    col_sel[dc_idx.reshape(-1), np.arange(GRID_W * GRID_W)] = 1.0
    valid = col_valid.reshape(1, -1).astype(np.float32)
    rpb_p = jnp.zeros((H, pr, pc), F32).at[:, :n_dr, :n_dc].set(rpb.astype(F32)).at[:, mask_row, :].set(MASK_VALUE)
    tab = pl.pallas_call(
        _bias_table_kernel,
        grid=(H,),
        in_specs=[
            pl.BlockSpec((1, pr, pc), lambda h: (h, 0, 0)),
            pl.BlockSpec((n_rows, pr), lambda h: (0, 0)),
            pl.BlockSpec((pc, GRID_W * GRID_W), lambda h: (0, 0)),
            pl.BlockSpec((1, GRID_W * GRID_W), lambda h: (0, 0)),
        ],
        out_specs=pl.BlockSpec((1, n_rows, GRID_W * GRID_W), lambda h: (h, 0, 0)),
        out_shape=jax.ShapeDtypeStruct((H, n_rows, GRID_W * GRID_W), F32),
        compiler_params=_params(("parallel",)),
        name="na_bias_table",
    )(rpb_p, jnp.asarray(row_sel), jnp.asarray(col_sel), jnp.asarray(valid))
    tab = tab.reshape(H, NA_VARIANTS, NA_GROUP, win, GRID_W, GRID_W)
    return tab.transpose(0, 1, 2, 4, 3, 5).reshape(H, NA_VARIANTS, NA_GROUP * GRID_W, win * GRID_W)


def neighborhood_attention(proj, kvc, bias, kh):
    B, S, _ = proj.shape
    L = kvc.shape[1]
    rows = S // GRID_W
    assert rows % NA_GROUP == 0 and rows // NA_GROUP >= NA_VARIANTS
    qb, kb, vb = 0, Q_COLS // HEAD_DIM, (Q_COLS + NA_COLS) // HEAD_DIM
    kern = functools.partial(_na_kernel, rows=rows, kh=kh, scale=HEAD_DIM ** -0.5 * LOG2_E)
    return pl.pallas_call(
        kern,
        grid=(B, NA_HEADS),
        in_specs=[
            pl.BlockSpec((1, S, HEAD_DIM), lambda b, h: (b, 0, qb + h)),
            pl.BlockSpec((1, S, HEAD_DIM), lambda b, h: (b, 0, kb + h)),
            pl.BlockSpec((1, S, HEAD_DIM), lambda b, h: (b, 0, vb + h)),
            pl.BlockSpec((1, L, HEAD_DIM), lambda b, h: (b, 0, h)),
            pl.BlockSpec((1, L, HEAD_DIM), lambda b, h: (b, 0, NA_HEADS + h)),
            pl.BlockSpec((1,) + bias.shape[1:], lambda b, h: (h, 0, 0, 0)),
        ],
        out_specs=pl.BlockSpec((1, S, HEAD_DIM), lambda b, h: (b, 0, h)),
        out_shape=jax.ShapeDtypeStruct((B, S, NA_COLS), BF16),
        compiler_params=_params(("parallel", "parallel")),
        name="na_attention",
    )(proj, proj, proj, kvc, kvc, bias)


def _gqa_kernel(*refs, subtract_max, side):
    n = len(side.arrays) if side else 0
    q_ref, k_ref, v_ref, kc_ref, vc_ref = refs[:5]
    o_ref = refs[5 + n]
    vx_ref, vcx_ref = refs[-2:]
    if side:
        side.run(refs[5:5 + n], refs[6 + n:6 + 2 * n])

    @pl.when(pl.program_id(2) == 0)
    def _():
        vx_ref[:, :HEAD_DIM] = v_ref[0]
        vx_ref[:, HEAD_DIM:] = jnp.ones_like(v_ref[0])
        vcx_ref[:, :HEAD_DIM] = vc_ref[0]
        vcx_ref[:, HEAD_DIM:] = jnp.ones_like(vc_ref[0])

    k = k_ref[0]
    kc = kc_ref[0]
    vx = vx_ref[...]
    vcx = vcx_ref[...]
    for r in range(GQA_GROUP):
        sl = slice(r * HEAD_DIM, (r + 1) * HEAD_DIM)
        q = q_ref[0, :, sl]
        s_ctx = lax.dot_general(q, kc, NT_DIMS, preferred_element_type=F32)
        s_lat = lax.dot_general(q, k, NT_DIMS, preferred_element_type=F32)
        if subtract_max:
            m = jnp.maximum(jnp.max(s_ctx, axis=-1, keepdims=True), jnp.max(s_lat, axis=-1, keepdims=True))
            s_ctx = s_ctx - m
            s_lat = s_lat - m
        p_ctx = jnp.exp2(s_ctx).astype(BF16)
        p_lat = jnp.exp2(s_lat).astype(BF16)
        ol = (jnp.dot(p_ctx, vcx, preferred_element_type=F32) + jnp.dot(p_lat, vx, preferred_element_type=F32))
        o_ref[0, :, sl] = (ol[:, :HEAD_DIM] / ol[:, HEAD_DIM:]).astype(o_ref.dtype)


GQA_Q_TILE = 256


def gqa_grid(B, S):
    return (B, GQA_KV_HEADS, S // _tile(S, GQA_Q_TILE))


def gqa_attention(qn, kn, proj, kcn, kvc, *side_cast, subtract_max):
    B, S, _ = qn.shape
    L = kcn.shape[1]
    tq = _tile(S, GQA_Q_TILE)
    gw = GQA_GROUP * HEAD_DIM
    vblk = (Q_COLS + 2 * NA_COLS + GQA_KV_COLS) // HEAD_DIM
    vcblk = (2 * NA_COLS + GQA_KV_COLS) // HEAD_DIM
    grid = gqa_grid(B, S)
    side = SideCast(side_cast, grid) if side_cast else None
    return pl.pallas_call(
        functools.partial(_gqa_kernel, subtract_max=subtract_max, side=side),
        grid=grid,
        in_specs=[
            pl.BlockSpec((1, tq, gw), lambda b, g, i: (b, i, g)),
            pl.BlockSpec((1, S, HEAD_DIM), lambda b, g, i: (b, 0, g)),
            pl.BlockSpec((1, S, HEAD_DIM), lambda b, g, i: (b, 0, vblk + g)),
            pl.BlockSpec((1, L, HEAD_DIM), lambda b, g, i: (b, 0, g)),
            pl.BlockSpec((1, L, HEAD_DIM), lambda b, g, i: (b, 0, vcblk + g)),
        ] + (side.specs() if side else []),
        out_specs=[pl.BlockSpec((1, tq, gw), lambda b, g, i: (b, i, g))] + (side.specs() if side else []),
        out_shape=[jax.ShapeDtypeStruct((B, S, GQA_Q_COLS), BF16)] + (side.out_shapes() if side else []),
        scratch_shapes=[pltpu.VMEM((S, 2 * HEAD_DIM), BF16), pltpu.VMEM((L, 2 * HEAD_DIM), BF16)],
        compiler_params=_params(("parallel", "parallel", "arbitrary")),
        name="gqa_attention" if subtract_max else "gqa_attention_bounded",
    )(qn, kn, proj, kcn, kvc, *side_cast)


def _mm_res_kernel(*refs, k_sizes):
    n = len(k_sizes)
    w_ref, x_ref, gate_ref, o_ref = refs[n:]
    acc = None
    off = 0
    for a_ref, ks in zip(refs[:n], k_sizes):
        part = jnp.dot(a_ref[0], w_ref[off:off + ks, :], preferred_element_type=F32)
        acc = part if acc is None else acc + part
        off += ks
    o_ref[0] = x_ref[0] + gate_ref[0] * acc


def matmul_residual(lhs_list, w, x, gate, *, tm=512, tn=2048):
    B, S, N = x.shape
    K = w.shape[0]
    k_sizes = tuple(a.shape[-1] for a in lhs_list)
    assert sum(k_sizes) == K
    tm = _tile(S, tm)
    tn = _tile(N, tn)
    lhs_specs = [pl.BlockSpec((1, tm, ks), lambda b, i, j: (b, i, 0)) for ks in k_sizes]
    return pl.pallas_call(
        functools.partial(_mm_res_kernel, k_sizes=k_sizes),
        grid=(B, S // tm, N // tn),
        in_specs=lhs_specs + [
            pl.BlockSpec((K, tn), lambda b, i, j: (0, j)),
            pl.BlockSpec((1, tm, tn), lambda b, i, j: (b, i, j)),
            pl.BlockSpec((1, 1, tn), lambda b, i, j: (b, 0, j)),
        ],
        out_specs=pl.BlockSpec((1, tm, tn), lambda b, i, j: (b, i, j)),
        out_shape=jax.ShapeDtypeStruct((B, S, N), F32),
        compiler_params=_params(("parallel", "parallel", "arbitrary")),
        name="matmul_residual",
    )(*lhs_list, w, x, gate)


DOWN_CHUNK = 512


def _accumulate_down(acc_ref, a, wd_ref):
    n = acc_ref.shape[1]
    step = DOWN_CHUNK if n % DOWN_CHUNK == 0 else n
    for n0 in range(0, n, step):
        acc_ref[:, n0:n0 + step] += jnp.dot(a, wd_ref[:, n0:n0 + step], preferred_element_type=F32)


def _ffn_kernel(x_ref, g_ref, sh_ref, sc_ref, gate_ref, wg_ref, wu_ref, wd_ref, o_ref, h_ref, acc_ref):
    j = pl.program_id(2)

    @pl.when(j == 0)
    def _():
        h_ref[...] = _norm_mod(x_ref[0], g_ref[...], sh_ref[0], sc_ref[0]).astype(BF16)
        acc_ref[...] = jnp.zeros_like(acc_ref)

    h = h_ref[...]
    a = (jax.nn.silu(jnp.dot(h, wg_ref[...], preferred_element_type=F32))
         * jnp.dot(h, wu_ref[...], preferred_element_type=F32)).astype(BF16)
    _accumulate_down(acc_ref, a, wd_ref)

    @pl.when(j == pl.num_programs(2) - 1)
    def _():
        o_ref[0] = x_ref[0] + gate_ref[0] * acc_ref[...]


def dense_ffn(x, g, shift, scale, gate, wg, wu, wd, *, tm=512, tf=512):
    B, S, D = x.shape
    FF = wg.shape[1]
    tm = _tile(S, tm)
    tf = _tile(FF, tf)
    vec = pl.BlockSpec((1, 1, D), lambda b, i, j: (b, 0, 0))
    return pl.pallas_call(
        _ffn_kernel,
        grid=(B, S // tm, FF // tf),
        in_specs=[
            pl.BlockSpec((1, tm, D), lambda b, i, j: (b, i, 0)),
            pl.BlockSpec((1, D), lambda b, i, j: (0, 0)),
            vec, vec, vec,
            pl.BlockSpec((D, tf), lambda b, i, j: (0, j)),
            pl.BlockSpec((D, tf), lambda b, i, j: (0, j)),
            pl.BlockSpec((tf, D), lambda b, i, j: (j, 0)),
        ],
        out_specs=pl.BlockSpec((1, tm, D), lambda b, i, j: (b, i, 0)),
        out_shape=jax.ShapeDtypeStruct((B, S, D), F32),
        scratch_shapes=[pltpu.VMEM((tm, D), BF16), pltpu.VMEM((tm, D), F32)],
        compiler_params=_params(("parallel", "parallel", "arbitrary")),
        name="dense_ffn",
    )(x, g.reshape(1, D), shift, scale, gate, wg, wu, wd)


def _conv_kernel(x_ref, xp_ref, xn_ref, g_ref, sh_ref, sc_ref, wb_ref, wc_ref, wx_ref, cw_ref,
                 o_ref, h_ref):
    i = pl.program_id(1)
    tm = x_ref.shape[1]

    @pl.when(pl.program_id(2) == 0)
    def _():
        g, sh, sc = g_ref[...], sh_ref[0], sc_ref[0]
        h_ref[:tm, :] = _norm_mod(x_ref[0], g, sh, sc).astype(BF16)
        halo = jnp.concatenate([xp_ref[0], xn_ref[0]], axis=0)
        h_ref[tm:, :] = _norm_mod(halo, g, sh, sc).astype(BF16)

    h = h_ref[...]
    bg = jnp.dot(h_ref[:tm, :], wb_ref[...], preferred_element_type=F32)
    ux = (jnp.dot(h, wc_ref[...], preferred_element_type=F32)
          * jnp.dot(h, wx_ref[...], preferred_element_type=F32))
    u = ux[:tm]
    uh = ux[tm:]
    prev = jnp.where(i > 0, uh[SUBLANES - 1:SUBLANES, :], 0.0)
    nxt = jnp.where(i < pl.num_programs(1) - 1, uh[SUBLANES:SUBLANES + 1, :], 0.0)
    row = lax.broadcasted_iota(I32, u.shape, 0)
    up = jnp.where(row == 0, prev, pltpu.roll(u, 1, 0))
    dn = jnp.where(row == tm - 1, nxt, pltpu.roll(u, tm - 1, 0))
    cw = cw_ref[...]
    y = cw[0:1, :] * up + cw[1:2, :] * u + cw[2:3, :] * dn
    o_ref[0] = (bg * y).astype(o_ref.dtype)


def conv_mixer_inner(x, g, shift, scale, w_in, conv_w, *, tm=1024, tc=512):
    B, S, D = x.shape
    tm = _tile(S, tm)
    tc = _tile(D, tc)
    nc = D // tc
    hb = tm // SUBLANES
    last_hb = S // SUBLANES - 1
    vec = pl.BlockSpec((1, 1, D), lambda b, i, j: (b, 0, 0))
    return pl.pallas_call(
        _conv_kernel,
        grid=(B, S // tm, nc),
        in_specs=[
            pl.BlockSpec((1, tm, D), lambda b, i, j: (b, i, 0)),
            pl.BlockSpec((1, SUBLANES, D), lambda b, i, j: (b, jnp.maximum(i * hb - 1, 0), 0)),
            pl.BlockSpec((1, SUBLANES, D), lambda b, i, j: (b, jnp.minimum((i + 1) * hb, last_hb), 0)),
            pl.BlockSpec((1, D), lambda b, i, j: (0, 0)),
            vec, vec,
            pl.BlockSpec((D, tc), lambda b, i, j: (0, j)),
            pl.BlockSpec((D, tc), lambda b, i, j: (0, nc + j)),
            pl.BlockSpec((D, tc), lambda b, i, j: (0, 2 * nc + j)),
            pl.BlockSpec((3, tc), lambda b, i, j: (0, j)),
        ],
        out_specs=pl.BlockSpec((1, tm, tc), lambda b, i, j: (b, i, j)),
        out_shape=jax.ShapeDtypeStruct((B, S, D), BF16),
        scratch_shapes=[pltpu.VMEM((tm + 2 * SUBLANES, D), BF16)],
        compiler_params=_params(("parallel", "parallel", "arbitrary")),
        name="conv_mixer",
    )(x, x, x, g.reshape(1, D), shift, scale, w_in, w_in, w_in, conv_w)


def _router_kernel(x_ref, g_ref, sh_ref, sc_ref, wr_ref, hp_ref, idx_ref, gcol_ref, cnt_ref, run_ref):
    tm, D = x_ref.shape[1], x_ref.shape[2]
    d2 = D // 2

    @pl.when((pl.program_id(0) == 0) & (pl.program_id(1) == 0))
    def _():
        run_ref[...] = jnp.zeros_like(run_ref)

    h = _norm_mod(x_ref[0], g_ref[...], sh_ref[0], sc_ref[0])

    lo = pltpu.bitcast(h[:, :d2].astype(BF16).astype(F32), U32)
    hi = pltpu.bitcast(h[:, d2:].astype(BF16).astype(F32), U32)
    hp_ref[0] = (hi & jnp.uint32(0xFFFF0000)) | (lo >> 16)

    logits = jnp.dot(h, wr_ref[...], preferred_element_type=F32, precision=lax.Precision.HIGHEST)
    lane = lax.broadcasted_iota(I32, logits.shape, 1).astype(F32)
    neg = jnp.float32(-jnp.inf)
    l1 = jnp.where(lane < N_EXPERTS, logits, neg)
    m1 = jnp.max(l1, axis=-1, keepdims=True)
    i1 = jnp.min(jnp.where(l1 == m1, lane, float(LANES)), axis=-1, keepdims=True)
    l2 = jnp.where(lane == i1, neg, l1)
    m2 = jnp.max(l2, axis=-1, keepdims=True)
    i2 = jnp.min(jnp.where(l2 == m2, lane, float(LANES)), axis=-1, keepdims=True)
    e = jnp.exp(m2 - m1)
    den = 1.0 + e
    gcol_ref[...] = jnp.where(lane == 0, 1.0 / den, jnp.where(lane == 1, e / den, 0.0))

    chosen = jnp.where((lane == i1) | (lane == i2), 1.0, 0.0)
    chosen_t = chosen.T
    rr = lax.broadcasted_iota(I32, (tm, tm), 0)
    cc = lax.broadcasted_iota(I32, (tm, tm), 1)
    before = jnp.where(rr < cc, 1.0, 0.0).astype(BF16)
    cum = jnp.dot(chosen_t.astype(BF16), before, preferred_element_type=F32) + run_ref[...]
    sel = jnp.where(lane == 0, i1, jnp.where(lane == 1, i2, 0.0)).T
    i1r = sel[0:1, :]
    i2r = sel[1:2, :]
    sub = lax.broadcasted_iota(I32, cum.shape, 0).astype(F32)
    p1r = jnp.sum(jnp.where(sub == i1r, cum, 0.0), axis=0, keepdims=True)
    p2r = jnp.sum(jnp.where(sub == i2r, cum, 0.0), axis=0, keepdims=True)
    idx_ref[0] = jnp.concatenate([i1r, i2r, p1r, p2r], axis=0).astype(I32)

    run_ref[...] += jnp.sum(chosen_t, axis=1, keepdims=True)
    cnt_ref[...] = jnp.broadcast_to(run_ref[...], cnt_ref.shape)


def moe_router(x, g, shift, scale, w_router, *, tm=512):
    B, S, D = x.shape
    E = w_router.shape[1]
    tm = _tile(S, tm)
    nS = S // tm
    wr = jnp.zeros((D, LANES), F32).at[:, :E].set(w_router.astype(F32))
    vec = pl.BlockSpec((1, 1, D), lambda b, i: (b, 0, 0))
    hp, idx, gcol, cnt = pl.pallas_call(
        _router_kernel,
        grid=(B, nS),
        in_specs=[
            pl.BlockSpec((1, tm, D), lambda b, i: (b, i, 0)),
            pl.BlockSpec((1, D), lambda b, i: (0, 0)),
            vec, vec,
            pl.BlockSpec((D, LANES), lambda b, i: (0, 0)),
        ],
        out_specs=[
            pl.BlockSpec((1, tm, D // 2), lambda b, i: (b, i, 0)),
            pl.BlockSpec((1, 4, tm), lambda b, i: (b * nS + i, 0, 0)),
            pl.BlockSpec((tm, LANES), lambda b, i: (b * nS + i, 0)),
            pl.BlockSpec((LANES, LANES), lambda b, i: (0, 0)),
        ],
        out_shape=[
            jax.ShapeDtypeStruct((B, S, D // 2), U32),
            jax.ShapeDtypeStruct((B * nS, 4, tm), I32),
            jax.ShapeDtypeStruct((B * S, LANES), F32),
            jax.ShapeDtypeStruct((LANES, LANES), F32),
        ],
        scratch_shapes=[pltpu.VMEM((LANES, 1), F32)],
        compiler_params=_params(("arbitrary", "arbitrary")),
        name="moe_router",
    )(x, g.reshape(1, D), shift, scale, wr)
    return hp.reshape(B * S, D // 2), idx, gcol, cnt


def _row_copy(src, dst, sem):
    return pltpu.make_async_copy(src, dst, sem)


ROW_DMA_UNROLL = 8
ROW_WAIT_UNROLL = 64


def _wait_rows(n, src_row, dst_row, sem):
    def wait(i, c):
        for _ in range(ROW_WAIT_UNROLL):
            _row_copy(src_row, dst_row, sem).wait()
        return c

    assert n % ROW_WAIT_UNROLL == 0
    lax.fori_loop(0, n // ROW_WAIT_UNROLL, wait, 0)


def _dispatch_kernel(dest_ref, hp_ref, xs_in_ref, xs_ref, sem):
    del xs_in_ref
    tm = hp_ref.shape[0]

    def start(t, c):
        for k in range(2):
            _row_copy(hp_ref.at[pl.ds(t, 1)], xs_ref.at[pl.ds(dest_ref[0, k, t], 1)], sem).start()
        return c

    lax.fori_loop(0, tm, start, 0, unroll=ROW_DMA_UNROLL)
    _wait_rows(2 * tm, hp_ref.at[pl.ds(0, 1)], xs_ref.at[pl.ds(0, 1)], sem)


def moe_dispatch(dest, hp, n_slots):
    N, d2 = hp.shape
    nT, _, tm = dest.shape
    xs0 = jnp.zeros((n_slots, d2), U32)
    return pl.pallas_call(
        _dispatch_kernel,
        grid=(nT,),
        in_specs=[
            pl.BlockSpec((1, 2, tm), lambda t: (t, 0, 0), memory_space=pltpu.SMEM),
            pl.BlockSpec((tm, d2), lambda t: (t, 0)),
            pl.BlockSpec(memory_space=pl.ANY),
        ],
        out_specs=pl.BlockSpec(memory_space=pl.ANY),
        out_shape=jax.ShapeDtypeStruct((n_slots, d2), U32),
        scratch_shapes=[pltpu.SemaphoreType.DMA(())],
        input_output_aliases={2: 0},
        compiler_params=_params(("arbitrary",)),
        name="moe_dispatch",
    )(dest, hp, xs0)


def _moe_ffn_kernel(be_ref, nv_ref, xs_ref, wg_ref, wu_ref, wd_ref, y_ref, xb_ref):
    b = pl.program_id(0)
    j = pl.program_id(1)
    d2 = xs_ref.shape[1]

    @pl.when((b >= nv_ref[0]) & (j == 0))
    def _():
        y_ref[...] = jnp.zeros_like(y_ref)

    @pl.when(b < nv_ref[0])
    def _():
        @pl.when(j == 0)
        def _():
            w = xs_ref[...]
            xb_ref[:, :d2] = pltpu.bitcast(w << 16, F32).astype(BF16)
            xb_ref[:, d2:] = pltpu.bitcast(w & jnp.uint32(0xFFFF0000), F32).astype(BF16)

            y_ref[...] = jnp.zeros_like(y_ref)

        x = xb_ref[...]
        a = (jax.nn.silu(jnp.dot(x, wg_ref[0], preferred_element_type=F32))
             * jnp.dot(x, wu_ref[0], preferred_element_type=F32)).astype(BF16)
        _accumulate_down(y_ref, a, wd_ref.at[0])


def moe_expert_ffn(block_e, n_valid, xs, wg, wu, wd, *, tm, tf=512):
    n_slots, d2 = xs.shape
    D = 2 * d2
    FF = wg.shape[2]
    tf = _tile(FF, tf)
    nj = FF // tf
    nb = n_slots // tm

    def row_map(b, j, be, nv):
        return (jnp.minimum(b, nv[0] - 1), 0)

    def col_of(b, j, nv):
        return jnp.where(b < nv[0], j, nj - 1)

    grid_spec = pltpu.PrefetchScalarGridSpec(
        num_scalar_prefetch=2,
        grid=(nb, nj),
        in_specs=[
            pl.BlockSpec((tm, d2), row_map),
            pl.BlockSpec((1, D, tf), lambda b, j, be, nv: (be[b], 0, col_of(b, j, nv))),
            pl.BlockSpec((1, D, tf), lambda b, j, be, nv: (be[b], 0, col_of(b, j, nv))),
            pl.BlockSpec((1, tf, D), lambda b, j, be, nv: (be[b], col_of(b, j, nv), 0)),
        ],
        out_specs=pl.BlockSpec((tm, D), lambda b, j, be, nv: (b, 0)),
        scratch_shapes=[pltpu.VMEM((tm, D), BF16)],
    )
    return pl.pallas_call(
        _moe_ffn_kernel,
        grid_spec=grid_spec,
        out_shape=jax.ShapeDtypeStruct((n_slots, D), F32),
        compiler_params=_params(("arbitrary", "arbitrary")),
        name="moe_expert_ffn",
    )(block_e, n_valid, xs, wg, wu, wd)


def _combine_kernel(dest_ref, x_ref, gcol_ref, gate_ref, fg_ref, y_ref, o_ref, ya_ref, yb_ref, sem):
    tm = x_ref.shape[1]
    bufs = (ya_ref, yb_ref)

    def start(t, c):
        for k in range(2):
            _row_copy(y_ref.at[pl.ds(dest_ref[0, k, t], 1)], bufs[k].at[pl.ds(t, 1)], sem).start()
        return c

    lax.fori_loop(0, tm, start, 0, unroll=ROW_DMA_UNROLL)
    _wait_rows(2 * tm, y_ref.at[pl.ds(0, 1)], ya_ref.at[pl.ds(0, 1)], sem)

    moe = gcol_ref[:, 0:1] * ya_ref[...] + gcol_ref[:, 1:2] * yb_ref[...]
    x = x_ref[0] + gate_ref[0] * moe
    y = x * lax.rsqrt(jnp.mean(x * x, axis=-1, keepdims=True) + NORM_EPS)
    o_ref[0] = y * fg_ref[...]


def moe_combine(dest, x, gcol, gate, final_g, y_buf):
    B, S, D = x.shape
    nT, _, tm = dest.shape
    nS = S // tm
    return pl.pallas_call(
        _combine_kernel,
        grid=(B, nS),
        in_specs=[
            pl.BlockSpec((1, 2, tm), lambda b, i: (b * nS + i, 0, 0), memory_space=pltpu.SMEM),
            pl.BlockSpec((1, tm, D), lambda b, i: (b, i, 0)),
            pl.BlockSpec((tm, LANES), lambda b, i: (b * nS + i, 0)),
            pl.BlockSpec((1, 1, D), lambda b, i: (b, 0, 0)),
            pl.BlockSpec((1, D), lambda b, i: (0, 0)),
            pl.BlockSpec(memory_space=pl.ANY),
        ],
        out_specs=pl.BlockSpec((1, tm, D), lambda b, i: (b, i, 0)),
        out_shape=jax.ShapeDtypeStruct((B, S, D), F32),
        scratch_shapes=[pltpu.VMEM((tm, D), F32), pltpu.VMEM((tm, D), F32), pltpu.SemaphoreType.DMA(())],
        compiler_params=_params(("arbitrary", "arbitrary")),
        name="moe_combine",
    )(dest, x, gcol, gate, final_g.reshape(1, D), y_buf)


def moe_layout(cnt, idx, n_tokens, blk):
    counts = cnt[:N_EXPERTS, 0].astype(I32)
    padded = (counts + blk - 1) // blk * blk
    pad_end = jnp.cumsum(padded)
    pad_start = (pad_end - padded).astype(I32)
    n_blocks = (2 * n_tokens) // blk + N_EXPERTS
    n_valid = (pad_end[-1] // blk).astype(I32)
    starts = jnp.minimum(jnp.arange(n_blocks, dtype=I32), n_valid - 1) * blk
    block_e = jnp.minimum(jnp.sum(pad_end[None, :] <= starts[:, None], axis=1), N_EXPERTS - 1).astype(I32)
    e, pos = idx[:, 0:2, :], idx[:, 2:4, :]
    group_start = jnp.sum(jnp.where(e[..., None] == jnp.arange(N_EXPERTS, dtype=I32), pad_start, 0), axis=-1)
    dest = (pos + group_start).astype(I32)
    return dest, block_e, n_valid.reshape(1), n_blocks * blk


def rope_tables(n_tokens):
    t = jnp.arange(n_tokens)
    row = (t // GRID_W).astype(F32)
    col = (t % GRID_W).astype(F32)
    half = HEAD_DIM // 2
    freqs = ROPE_THETA ** (-jnp.arange(0, half, 2, dtype=F32) / half)
    ang = jnp.concatenate([row[:, None] * freqs, col[:, None] * freqs], axis=-1)
    cos = jnp.repeat(jnp.cos(ang), 2, axis=-1)
    sin = jnp.repeat(jnp.sin(ang), 2, axis=-1) * jnp.tile(jnp.array([-1.0, 1.0], F32), half)
    return cos, sin


def kernel(x, c, ctx, c_ctx, ada_w, ada_b, norm1_g, norm2_g, attn_w_in, attn_w_out, na_rpb, gqa_q_norm_g,
           gqa_k_norm_g, ffn_w_gate, ffn_w_up, ffn_w_down, conv_w_in, conv_w, conv_w_out, moe_w_router,
           moe_w_gate, moe_w_up, moe_w_down, final_norm_g):
    B, S, D = x.shape
    assert ada_w.shape[0] == 2 and S % GRID_W == 0 and S // GRID_W >= WIN_H
    kh = WIN_H

    n_cond = -(-(B + 1) // SUBLANES) * SUBLANES
    cond = jnp.zeros((n_cond, D), F32).at[:B].set(c).at[B].set(c_ctx)
    mod = adaln_terms(cond, ada_w, ada_b)

    def terms(layer, rows):
        m = mod[layer, rows].reshape(-1, 6, D)
        return [m[:, k][:, None, :] for k in range(6)]

    sh1, sc1, g1, sh2, sc2, g2 = terms(0, slice(0, B))
    csh1, csc1 = [jnp.broadcast_to(t, (B, 1, D)) for t in terms(0, slice(B, B + 1))[:2]]

    w_in = attn_w_in[0].astype(BF16)
    proj = norm_mod_matmul(x, norm1_g[0], sh1, sc1, w_in)
    kvc = norm_mod_matmul(ctx, norm1_g[0], csh1, csc1, w_in, col_off=Q_COLS, tm=256)
    cos, sin = rope_tables(S)
    q_scale = HEAD_DIM ** -0.5 * LOG2_E
    qn = head_norm(proj, NA_COLS // GQA_Q_COLS, GQA_Q_HEADS, gqa_q_norm_g[0], cos, sin, rope=True,
                   out_scale=q_scale)
    kn = head_norm(proj, (Q_COLS + 2 * NA_COLS) // GQA_KV_COLS, GQA_KV_HEADS, gqa_k_norm_g[0], cos, sin, rope=True)
    L = ctx.shape[1]
    kcn = head_norm(kvc, 2 * NA_COLS // GQA_KV_COLS, GQA_KV_HEADS, gqa_k_norm_g[0], cos[:L], sin[:L], rope=False,
                    tm=L)
    out_a = neighborhood_attention(proj, kvc, na_bias_table(na_rpb[0], kh), kh)
    score_bound = (HEAD_DIM * q_scale * BF16_ROUNDING_MARGIN
                   * jnp.max(jnp.abs(gqa_q_norm_g[0])) * jnp.max(jnp.abs(gqa_k_norm_g[0])))
    E, _, FF = moe_w_gate[0].shape
    moe_f32 = [moe_w_gate[0].reshape(E * D, FF), moe_w_up[0].reshape(E * D, FF), moe_w_down[0].reshape(E * FF, D)]
    if not SideCast.fits(moe_f32, gqa_grid(B, S)):
        moe_bf16, moe_f32 = [a.astype(BF16) for a in moe_f32], []
    out_b, *converted = lax.cond(
        score_bound <= GQA_UNSHIFTED_SCORE_LIMIT,
        functools.partial(gqa_attention, subtract_max=False),
        functools.partial(gqa_attention, subtract_max=True),
        qn, kn, proj, kcn, kvc, *moe_f32)
    moe_wg, moe_wu, moe_wd = converted if moe_f32 else moe_bf16
    x = matmul_residual([out_a, out_b], attn_w_out[0].astype(BF16), x, g1)

    x = dense_ffn(x, norm2_g[0], sh2, sc2, g2, ffn_w_gate[0].astype(BF16), ffn_w_up[0].astype(BF16),
                  ffn_w_down[0].astype(BF16))

    sh1, sc1, g1, sh2, sc2, g2 = terms(1, slice(0, B))
    z = conv_mixer_inner(x, norm1_g[1], sh1, sc1, conv_w_in[0].astype(BF16), conv_w[0])
    x = matmul_residual([z], conv_w_out[0].astype(BF16), x, g1)

    moe_blk = 1024 if (2 * B * S) % 1024 == 0 else 2 * B * S // N_EXPERTS
    hp, idx, gcol, cnt = moe_router(x, norm2_g[1], sh2, sc2, moe_w_router[0])
    dest, block_e, n_valid, n_slots = moe_layout(cnt, idx, B * S, moe_blk)
    xs = moe_dispatch(dest, hp, n_slots)
    y_buf = moe_expert_ffn(block_e, n_valid, xs, moe_wg.reshape(E, D, FF), moe_wu.reshape(E, D, FF),
                           moe_wd.reshape(E, FF, D), tm=moe_blk)
    return moe_combine(dest, x, gcol, g2, final_norm_g, y_buf)
```

```python
import functools

import numpy as np
import jax
import jax.numpy as jnp
from jax import lax
from jax.experimental import pallas as pl
from jax.experimental.pallas import tpu as pltpu

F32 = jnp.float32
BF16 = jnp.bfloat16
I32 = jnp.int32
U32 = jnp.uint32

GRID_W = 64
HEAD_DIM = 128
NA_HEADS = 8
GQA_Q_HEADS = 8
GQA_KV_HEADS = 2
GQA_GROUP = GQA_Q_HEADS // GQA_KV_HEADS
WIN_H = 8
WIN_W = 16
ROPE_THETA = 10000.0
N_EXPERTS = 8
NORM_EPS = 1e-6
NA_COLS = NA_HEADS * HEAD_DIM
GQA_Q_COLS = GQA_Q_HEADS * HEAD_DIM
GQA_KV_COLS = GQA_KV_HEADS * HEAD_DIM
Q_COLS = NA_COLS + GQA_Q_COLS
KV_COLS = 2 * NA_COLS + 2 * GQA_KV_COLS

LANES = 128
SUBLANES = 8
V7X_VMEM_LIMIT = 56 * 1024 * 1024
MASK_VALUE = -1e30
NT_DIMS = (((1,), (1,)), ((), ()))
LOG2_E = 1.4426950408889634
GQA_UNSHIFTED_SCORE_LIMIT = 64.0
BF16_ROUNDING_MARGIN = 1.02


def _tile(dim, pref):
    return pref if dim % pref == 0 else dim


def _params(sem, vmem=V7X_VMEM_LIMIT):
    return pltpu.CompilerParams(dimension_semantics=sem, vmem_limit_bytes=vmem)


def _norm_mod(x, g, shift, scale):
    y = x * lax.rsqrt(jnp.mean(x * x, axis=-1, keepdims=True) + NORM_EPS)
    return y * (g * (1.0 + scale)) + shift


def _adaln_kernel(cond_ref, w_ref, b_ref, o_ref):
    a = jax.nn.silu(cond_ref[...])
    o_ref[0] = jnp.dot(a, w_ref[0], preferred_element_type=F32,
                       precision=lax.Precision.HIGHEST) + b_ref[0]


def adaln_terms(cond, ada_w, ada_b):
    L, D, N = ada_w.shape
    R = cond.shape[0]
    tn = _tile(N, 1024)
    return pl.pallas_call(
        _adaln_kernel,
        grid=(L, N // tn),
        in_specs=[
            pl.BlockSpec((R, D), lambda l, j: (0, 0)),
            pl.BlockSpec((1, D, tn), lambda l, j: (l, 0, j)),
            pl.BlockSpec((1, 1, tn), lambda l, j: (l, 0, j)),
        ],
        out_specs=pl.BlockSpec((1, R, tn), lambda l, j: (l, 0, j)),
        out_shape=jax.ShapeDtypeStruct((L, R, N), F32),
        compiler_params=_params(("parallel", "parallel")),
        name="adaln",
    )(cond, ada_w, ada_b.reshape(L, 1, N))


BF16_SUBLANES = 16


class SideCast:
    def __init__(self, arrays, grid):
        self.arrays = list(arrays)
        self.grid = grid
        B, nI, nJ = grid
        self.plans = []
        for a in self.arrays:
            R = a.shape[0]
            fits = [nj for nj in range(nJ, 0, -1)
                    if R % (B * nI * nj) == 0 and (R // (B * nI * nj)) % BF16_SUBLANES == 0]
            assert fits, "no even split of the rows over the grid"
            self.plans.append((fits[0], R // (B * nI * fits[0])))

    @staticmethod
    def fits(arrays, grid):
        try:
            SideCast(arrays, grid)
            return True
        except AssertionError:
            return False

    def specs(self):
        _, nI, _ = self.grid

        def spec(a, nj, rows):
            return pl.BlockSpec((rows, a.shape[1]), lambda b, i, j: ((b * nI + i) * nj + jnp.minimum(j, nj - 1), 0))

        return [spec(a, nj, rows) for a, (nj, rows) in zip(self.arrays, self.plans)]

    def out_shapes(self):
        return [jax.ShapeDtypeStruct(a.shape, BF16) for a in self.arrays]

    def run(self, src_refs, dst_refs):
        for src, dst in zip(src_refs, dst_refs):
            dst[...] = src[...].astype(BF16)


def _nm_matmul_kernel(x_ref, g_ref, sh_ref, sc_ref, w_ref, o_ref, h_ref):
    @pl.when(pl.program_id(2) == 0)
    def _():
        h_ref[...] = _norm_mod(x_ref[0], g_ref[...], sh_ref[0], sc_ref[0]).astype(BF16)

    o_ref[0] = jnp.dot(h_ref[...], w_ref[...], preferred_element_type=F32).astype(o_ref.dtype)


def norm_mod_matmul(x, g, shift, scale, w, *, col_off=0, n_cols=None, tm=1024, tn=512):
    B, S, D = x.shape
    n_cols = w.shape[1] - col_off if n_cols is None else n_cols
    tm = _tile(S, tm)
    tn = _tile(n_cols, tn)
    assert col_off % tn == 0
    joff = col_off // tn
    return pl.pallas_call(
        _nm_matmul_kernel,
        grid=(B, S // tm, n_cols // tn),
        in_specs=[
            pl.BlockSpec((1, tm, D), lambda b, i, j: (b, i, 0)),
            pl.BlockSpec((1, D), lambda b, i, j: (0, 0)),
            pl.BlockSpec((1, 1, D), lambda b, i, j: (b, 0, 0)),
            pl.BlockSpec((1, 1, D), lambda b, i, j: (b, 0, 0)),
            pl.BlockSpec((D, tn), lambda b, i, j: (0, j + joff)),
        ],
        out_specs=pl.BlockSpec((1, tm, tn), lambda b, i, j: (b, i, j)),
        out_shape=jax.ShapeDtypeStruct((B, S, n_cols), BF16),
        scratch_shapes=[pltpu.VMEM((tm, D), BF16)],
        compiler_params=_params(("parallel", "parallel", "arbitrary")),
        name="norm_mod_matmul",
    )(x, g.reshape(1, D), shift, scale, w)


def _head_norm_kernel(x_ref, g_ref, cos_ref, sin_ref, o_ref, *, n_heads, rope, out_scale):
    g = g_ref[...]
    for h in range(n_heads):
        sl = slice(h * HEAD_DIM, (h + 1) * HEAD_DIM)
        xh = x_ref[0, :, sl].astype(F32)
        y = xh * lax.rsqrt(jnp.mean(xh * xh, axis=-1, keepdims=True) + NORM_EPS) * g
        if rope:
            lane = lax.broadcasted_iota(I32, y.shape, 1)
            swap = jnp.where((lane & 1) == 0, pltpu.roll(y, HEAD_DIM - 1, 1), pltpu.roll(y, 1, 1))
            y = y * cos_ref[...] + swap * sin_ref[...]
        if out_scale != 1.0:
            y = y * out_scale
        o_ref[0, :, sl] = y.astype(o_ref.dtype)


def head_norm(src, col_block, n_heads, g, cos, sin, *, rope, out_scale=1.0, tm=512):
    B, S, _ = src.shape
    w = n_heads * HEAD_DIM
    tm = _tile(S, tm)
    kern = functools.partial(_head_norm_kernel, n_heads=n_heads, rope=rope, out_scale=out_scale)
    return pl.pallas_call(
        kern,
        grid=(B, S // tm),
        in_specs=[
            pl.BlockSpec((1, tm, w), lambda b, i: (b, i, col_block)),
            pl.BlockSpec((1, HEAD_DIM), lambda b, i: (0, 0)),
            pl.BlockSpec((tm, HEAD_DIM), lambda b, i: (i, 0)),
            pl.BlockSpec((tm, HEAD_DIM), lambda b, i: (i, 0)),
        ],
        out_specs=pl.BlockSpec((1, tm, w), lambda b, i: (b, i, 0)),
        out_shape=jax.ShapeDtypeStruct((B, S, w), BF16),
        compiler_params=_params(("parallel", "parallel")),
        name="head_norm",
    )(src, g.reshape(1, HEAD_DIM), cos, sin)


NA_GROUP = 4
NA_GROUP_UNROLL = 4
NA_VARIANTS = 3


def _na_kernel(q_ref, k_ref, v_ref, kc_ref, vc_ref, bias_ref, o_ref, *, rows, kh, scale):
    kc = kc_ref[0]
    vc = vc_ref[0]
    n_groups = rows // NA_GROUP
    win = NA_GROUP + kh
    gq = NA_GROUP * GRID_W

    def body(g, carry):
        w0 = jnp.clip(g * NA_GROUP - kh // 2, 0, rows - win)
        variant = jnp.where(g == 0, 0, jnp.where(g == n_groups - 1, 2, 1))
        qs = pl.ds(pl.multiple_of(g * gq, gq), gq)
        band = pl.ds(pl.multiple_of(w0 * GRID_W, GRID_W), win * GRID_W)
        q = q_ref[0, qs, :]
        kb = k_ref[0, band, :]
        vb = v_ref[0, band, :]
        s_loc = lax.dot_general(q, kb, NT_DIMS, preferred_element_type=F32) * scale + bias_ref[0, variant]
        s_ctx = lax.dot_general(q, kc, NT_DIMS, preferred_element_type=F32) * scale
        m = jnp.maximum(jnp.max(s_loc, axis=-1, keepdims=True), jnp.max(s_ctx, axis=-1, keepdims=True))
        p_loc = jnp.exp2(s_loc - m)
        p_ctx = jnp.exp2(s_ctx - m)
        l = jnp.sum(p_loc, axis=-1, keepdims=True) + jnp.sum(p_ctx, axis=-1, keepdims=True)
        o = (jnp.dot(p_ctx.astype(BF16), vc, preferred_element_type=F32)
             + jnp.dot(p_loc.astype(BF16), vb, preferred_element_type=F32))
        o_ref[0, qs, :] = (o / l).astype(o_ref.dtype)
        return carry

    lax.fori_loop(0, n_groups, body, 0, unroll=NA_GROUP_UNROLL)


def _bias_table_kernel(rpb_ref, row_sel_ref, col_sel_ref, valid_ref, o_ref):
    rows = jnp.dot(row_sel_ref[...], rpb_ref[0], preferred_element_type=F32, precision=lax.Precision.HIGHEST)
    tab = jnp.dot(rows, col_sel_ref[...], preferred_element_type=F32, precision=lax.Precision.HIGHEST)
    o_ref[0] = jnp.where(valid_ref[...] > 0.0, tab * LOG2_E, MASK_VALUE)


def na_bias_table(rpb, kh):
    H, n_dr, n_dc = rpb.shape
    win = NA_GROUP + kh
    cols = np.arange(GRID_W)
    col_start = np.clip(cols - WIN_W // 2, 0, GRID_W - WIN_W)
    col_valid = (cols[None, :] >= col_start[:, None]) & (cols[None, :] < col_start[:, None] + WIN_W)
    dc_idx = np.clip(cols[None, :] - cols[:, None] + WIN_W - 1, 0, 2 * WIN_W - 2)
    ql = np.arange(NA_GROUP)[:, None]
    wr = np.arange(win)[None, :]
    band0 = [np.zeros_like(ql), ql, np.full_like(ql, win - kh)]
    dr = [wr - ql + WIN_H - 1, wr - ql + WIN_H - 1 - kh // 2, wr - ql + WIN_H - 1 - kh]
    pr = -(-(n_dr + 1) // SUBLANES) * SUBLANES
    pc = -(-n_dc // SUBLANES) * SUBLANES
    mask_row = pr - 1
    n_rows = NA_VARIANTS * NA_GROUP * win
    row_pick = np.concatenate([np.where((wr >= b0) & (wr < b0 + kh), d, mask_row).reshape(-1)
                               for b0, d in zip(band0, dr)])
    assert row_pick.min() >= 0 and (row_pick[row_pick != mask_row] < n_dr).all()
    row_sel = np.zeros((n_rows, pr), np.float32)
    row_sel[np.arange(n_rows), row_pick] = 1.0
    col_sel = np.zeros((pc, GRID_W * GRID_W), np.float32)
    col_sel[dc_idx.reshape(-1), np.arange(GRID_W * GRID_W)] = 1.0
    valid = col_valid.reshape(1, -1).astype(np.float32)
    rpb_p = jnp.zeros((H, pr, pc), F32).at[:, :n_dr, :n_dc].set(rpb.astype(F32)).at[:, mask_row, :].set(MASK_VALUE)
    tab = pl.pallas_call(
        _bias_table_kernel,
        grid=(H,),
        in_specs=[
            pl.BlockSpec((1, pr, pc), lambda h: (h, 0, 0)),
            pl.BlockSpec((n_rows, pr), lambda h: (0, 0)),
            pl.BlockSpec((pc, GRID_W * GRID_W), lambda h: (0, 0)),
            pl.BlockSpec((1, GRID_W * GRID_W), lambda h: (0, 0)),
        ],
        out_specs=pl.BlockSpec((1, n_rows, GRID_W * GRID_W), lambda h: (h, 0, 0)),
        out_shape=jax.ShapeDtypeStruct((H, n_rows, GRID_W * GRID_W), F32),
        compiler_params=_params(("parallel",)),
        name="na_bias_table",
    )(rpb_p, jnp.asarray(row_sel), jnp.asarray(col_sel), jnp.asarray(valid))
    tab = tab.reshape(H, NA_VARIANTS, NA_GROUP, win, GRID_W, GRID_W)
    return tab.transpose(0, 1, 2, 4, 3, 5).reshape(H, NA_VARIANTS, NA_GROUP * GRID_W, win * GRID_W)


def neighborhood_attention(proj, kvc, bias, kh):
    B, S, _ = proj.shape
    L = kvc.shape[1]
    rows = S // GRID_W
    assert rows % NA_GROUP == 0 and rows // NA_GROUP >= NA_VARIANTS
    qb, kb, vb = 0, Q_COLS // HEAD_DIM, (Q_COLS + NA_COLS) // HEAD_DIM
    kern = functools.partial(_na_kernel, rows=rows, kh=kh, scale=HEAD_DIM ** -0.5 * LOG2_E)
    return pl.pallas_call(
        kern,
        grid=(B, NA_HEADS),
        in_specs=[
            pl.BlockSpec((1, S, HEAD_DIM), lambda b, h: (b, 0, qb + h)),
            pl.BlockSpec((1, S, HEAD_DIM), lambda b, h: (b, 0, kb + h)),
            pl.BlockSpec((1, S, HEAD_DIM), lambda b, h: (b, 0, vb + h)),
            pl.BlockSpec((1, L, HEAD_DIM), lambda b, h: (b, 0, h)),
            pl.BlockSpec((1, L, HEAD_DIM), lambda b, h: (b, 0, NA_HEADS + h)),
            pl.BlockSpec((1,) + bias.shape[1:], lambda b, h: (h, 0, 0, 0)),
        ],
        out_specs=pl.BlockSpec((1, S, HEAD_DIM), lambda b, h: (b, 0, h)),
        out_shape=jax.ShapeDtypeStruct((B, S, NA_COLS), BF16),
        compiler_params=_params(("parallel", "parallel")),
        name="na_attention",
    )(proj, proj, proj, kvc, kvc, bias)


def _gqa_kernel(*refs, subtract_max, side):
    n = len(side.arrays) if side else 0
    q_ref, k_ref, v_ref, kc_ref, vc_ref = refs[:5]
    o_ref = refs[5 + n]
    vx_ref, vcx_ref = refs[-2:]
    if side:
        side.run(refs[5:5 + n], refs[6 + n:6 + 2 * n])

    @pl.when(pl.program_id(2) == 0)
    def _():
        vx_ref[:, :HEAD_DIM] = v_ref[0]
        vx_ref[:, HEAD_DIM:] = jnp.ones_like(v_ref[0])
        vcx_ref[:, :HEAD_DIM] = vc_ref[0]
        vcx_ref[:, HEAD_DIM:] = jnp.ones_like(vc_ref[0])

    k = k_ref[0]
    kc = kc_ref[0]
    vx = vx_ref[...]
    vcx = vcx_ref[...]
    for r in range(GQA_GROUP):
        sl = slice(r * HEAD_DIM, (r + 1) * HEAD_DIM)
        q = q_ref[0, :, sl]
        s_ctx = lax.dot_general(q, kc, NT_DIMS, preferred_element_type=F32)
        s_lat = lax.dot_general(q, k, NT_DIMS, preferred_element_type=F32)
        if subtract_max:
            m = jnp.maximum(jnp.max(s_ctx, axis=-1, keepdims=True), jnp.max(s_lat, axis=-1, keepdims=True))
            s_ctx = s_ctx - m
            s_lat = s_lat - m
        p_ctx = jnp.exp2(s_ctx).astype(BF16)
        p_lat = jnp.exp2(s_lat).astype(BF16)
        ol = (jnp.dot(p_ctx, vcx, preferred_element_type=F32) + jnp.dot(p_lat, vx, preferred_element_type=F32))
        o_ref[0, :, sl] = (ol[:, :HEAD_DIM] / ol[:, HEAD_DIM:]).astype(o_ref.dtype)


GQA_Q_TILE = 512


def gqa_grid(B, S):
    return (B, GQA_KV_HEADS, S // _tile(S, GQA_Q_TILE))


def gqa_attention(qn, kn, proj, kcn, kvc, *side_cast, subtract_max):
    B, S, _ = qn.shape
    L = kcn.shape[1]
    tq = _tile(S, GQA_Q_TILE)
    gw = GQA_GROUP * HEAD_DIM
    vblk = (Q_COLS + 2 * NA_COLS + GQA_KV_COLS) // HEAD_DIM
    vcblk = (2 * NA_COLS + GQA_KV_COLS) // HEAD_DIM
    grid = gqa_grid(B, S)
    side = SideCast(side_cast, grid) if side_cast else None
    return pl.pallas_call(
        functools.partial(_gqa_kernel, subtract_max=subtract_max, side=side),
        grid=grid,
        in_specs=[
            pl.BlockSpec((1, tq, gw), lambda b, g, i: (b, i, g)),
            pl.BlockSpec((1, S, HEAD_DIM), lambda b, g, i: (b, 0, g)),
            pl.BlockSpec((1, S, HEAD_DIM), lambda b, g, i: (b, 0, vblk + g)),
            pl.BlockSpec((1, L, HEAD_DIM), lambda b, g, i: (b, 0, g)),
            pl.BlockSpec((1, L, HEAD_DIM), lambda b, g, i: (b, 0, vcblk + g)),
        ] + (side.specs() if side else []),
        out_specs=[pl.BlockSpec((1, tq, gw), lambda b, g, i: (b, i, g))] + (side.specs() if side else []),
        out_shape=[jax.ShapeDtypeStruct((B, S, GQA_Q_COLS), BF16)] + (side.out_shapes() if side else []),
        scratch_shapes=[pltpu.VMEM((S, 2 * HEAD_DIM), BF16), pltpu.VMEM((L, 2 * HEAD_DIM), BF16)],
        compiler_params=_params(("parallel", "parallel", "arbitrary")),
        name="gqa_attention" if subtract_max else "gqa_attention_bounded",
    )(qn, kn, proj, kcn, kvc, *side_cast)


def _mm_res_kernel(*refs, k_sizes):
    n = len(k_sizes)
    w_ref, x_ref, gate_ref, o_ref = refs[n:]
    acc = None
    off = 0
    for a_ref, ks in zip(refs[:n], k_sizes):
        part = jnp.dot(a_ref[0], w_ref[off:off + ks, :], preferred_element_type=F32)
        acc = part if acc is None else acc + part
        off += ks
    o_ref[0] = x_ref[0] + gate_ref[0] * acc


def matmul_residual(lhs_list, w, x, gate, *, tm=512, tn=2048):
    B, S, N = x.shape
    K = w.shape[0]
    k_sizes = tuple(a.shape[-1] for a in lhs_list)
    assert sum(k_sizes) == K
    tm = _tile(S, tm)
    tn = _tile(N, tn)
    lhs_specs = [pl.BlockSpec((1, tm, ks), lambda b, i, j: (b, i, 0)) for ks in k_sizes]
    return pl.pallas_call(
        functools.partial(_mm_res_kernel, k_sizes=k_sizes),
        grid=(B, S // tm, N // tn),
        in_specs=lhs_specs + [
            pl.BlockSpec((K, tn), lambda b, i, j: (0, j)),
            pl.BlockSpec((1, tm, tn), lambda b, i, j: (b, i, j)),
            pl.BlockSpec((1, 1, tn), lambda b, i, j: (b, 0, j)),
        ],
        out_specs=pl.BlockSpec((1, tm, tn), lambda b, i, j: (b, i, j)),
        out_shape=jax.ShapeDtypeStruct((B, S, N), F32),
        compiler_params=_params(("parallel", "parallel", "arbitrary")),
        name="matmul_residual",
    )(*lhs_list, w, x, gate)


DOWN_CHUNK = 512


def _accumulate_down(acc_ref, a, wd_ref):
    n = acc_ref.shape[1]
    step = DOWN_CHUNK if n % DOWN_CHUNK == 0 else n
    for n0 in range(0, n, step):
        acc_ref[:, n0:n0 + step] += jnp.dot(a, wd_ref[:, n0:n0 + step], preferred_element_type=F32)


def _ffn_kernel(x_ref, g_ref, sh_ref, sc_ref, gate_ref, wg_ref, wu_ref, wd_ref, o_ref, h_ref, acc_ref):
    j = pl.program_id(2)

    @pl.when(j == 0)
    def _():
        h_ref[...] = _norm_mod(x_ref[0], g_ref[...], sh_ref[0], sc_ref[0]).astype(BF16)
        acc_ref[...] = jnp.zeros_like(acc_ref)

    h = h_ref[...]
    a = (jax.nn.silu(jnp.dot(h, wg_ref[...], preferred_element_type=F32))
         * jnp.dot(h, wu_ref[...], preferred_element_type=F32)).astype(BF16)
    _accumulate_down(acc_ref, a, wd_ref)

    @pl.when(j == pl.num_programs(2) - 1)
    def _():
        o_ref[0] = x_ref[0] + gate_ref[0] * acc_ref[...]


def dense_ffn(x, g, shift, scale, gate, wg, wu, wd, *, tm=512, tf=512):
    B, S, D = x.shape
    FF = wg.shape[1]
    tm = _tile(S, tm)
    tf = _tile(FF, tf)
    vec = pl.BlockSpec((1, 1, D), lambda b, i, j: (b, 0, 0))
    return pl.pallas_call(
        _ffn_kernel,
        grid=(B, S // tm, FF // tf),
        in_specs=[
            pl.BlockSpec((1, tm, D), lambda b, i, j: (b, i, 0)),
            pl.BlockSpec((1, D), lambda b, i, j: (0, 0)),
            vec, vec, vec,
            pl.BlockSpec((D, tf), lambda b, i, j: (0, j)),
            pl.BlockSpec((D, tf), lambda b, i, j: (0, j)),
            pl.BlockSpec((tf, D), lambda b, i, j: (j, 0)),
        ],
        out_specs=pl.BlockSpec((1, tm, D), lambda b, i, j: (b, i, 0)),
        out_shape=jax.ShapeDtypeStruct((B, S, D), F32),
        scratch_shapes=[pltpu.VMEM((tm, D), BF16), pltpu.VMEM((tm, D), F32)],
        compiler_params=_params(("parallel", "parallel", "arbitrary")),
        name="dense_ffn",
    )(x, g.reshape(1, D), shift, scale, gate, wg, wu, wd)


def _conv_kernel(x_ref, xp_ref, xn_ref, g_ref, sh_ref, sc_ref, wb_ref, wc_ref, wx_ref, cw_ref,
                 o_ref, h_ref):
    i = pl.program_id(1)
    tm = x_ref.shape[1]

    @pl.when(pl.program_id(2) == 0)
    def _():
        g, sh, sc = g_ref[...], sh_ref[0], sc_ref[0]
        h_ref[:tm, :] = _norm_mod(x_ref[0], g, sh, sc).astype(BF16)
        halo = jnp.concatenate([xp_ref[0], xn_ref[0]], axis=0)
        h_ref[tm:, :] = _norm_mod(halo, g, sh, sc).astype(BF16)

    h = h_ref[...]
    bg = jnp.dot(h_ref[:tm, :], wb_ref[...], preferred_element_type=F32)
    ux = (jnp.dot(h, wc_ref[...], preferred_element_type=F32)
          * jnp.dot(h, wx_ref[...], preferred_element_type=F32))
    u = ux[:tm]
    uh = ux[tm:]
    prev = jnp.where(i > 0, uh[SUBLANES - 1:SUBLANES, :], 0.0)
    nxt = jnp.where(i < pl.num_programs(1) - 1, uh[SUBLANES:SUBLANES + 1, :], 0.0)
    row = lax.broadcasted_iota(I32, u.shape, 0)
    up = jnp.where(row == 0, prev, pltpu.roll(u, 1, 0))
    dn = jnp.where(row == tm - 1, nxt, pltpu.roll(u, tm - 1, 0))
    cw = cw_ref[...]
    y = cw[0:1, :] * up + cw[1:2, :] * u + cw[2:3, :] * dn
    o_ref[0] = (bg * y).astype(o_ref.dtype)


def conv_mixer_inner(x, g, shift, scale, w_in, conv_w, *, tm=1024, tc=512):
    B, S, D = x.shape
    tm = _tile(S, tm)
    tc = _tile(D, tc)
    nc = D // tc
    hb = tm // SUBLANES
    last_hb = S // SUBLANES - 1
    vec = pl.BlockSpec((1, 1, D), lambda b, i, j: (b, 0, 0))
    return pl.pallas_call(
        _conv_kernel,
        grid=(B, S // tm, nc),
        in_specs=[
            pl.BlockSpec((1, tm, D), lambda b, i, j: (b, i, 0)),
            pl.BlockSpec((1, SUBLANES, D), lambda b, i, j: (b, jnp.maximum(i * hb - 1, 0), 0)),
            pl.BlockSpec((1, SUBLANES, D), lambda b, i, j: (b, jnp.minimum((i + 1) * hb, last_hb), 0)),
            pl.BlockSpec((1, D), lambda b, i, j: (0, 0)),
            vec, vec,
            pl.BlockSpec((D, tc), lambda b, i, j: (0, j)),
            pl.BlockSpec((D, tc), lambda b, i, j: (0, nc + j)),
            pl.BlockSpec((D, tc), lambda b, i, j: (0, 2 * nc + j)),
            pl.BlockSpec((3, tc), lambda b, i, j: (0, j)),
        ],
        out_specs=pl.BlockSpec((1, tm, tc), lambda b, i, j: (b, i, j)),
        out_shape=jax.ShapeDtypeStruct((B, S, D), BF16),
        scratch_shapes=[pltpu.VMEM((tm + 2 * SUBLANES, D), BF16)],
        compiler_params=_params(("parallel", "parallel", "arbitrary")),
        name="conv_mixer",
    )(x, x, x, g.reshape(1, D), shift, scale, w_in, w_in, w_in, conv_w)


def _router_kernel(x_ref, g_ref, sh_ref, sc_ref, wr_ref, hp_ref, idx_ref, gcol_ref, cnt_ref, run_ref):
    tm, D = x_ref.shape[1], x_ref.shape[2]
    d2 = D // 2

    @pl.when((pl.program_id(0) == 0) & (pl.program_id(1) == 0))
    def _():
        run_ref[...] = jnp.zeros_like(run_ref)

    h = _norm_mod(x_ref[0], g_ref[...], sh_ref[0], sc_ref[0])

    lo = pltpu.bitcast(h[:, :d2].astype(BF16).astype(F32), U32)
    hi = pltpu.bitcast(h[:, d2:].astype(BF16).astype(F32), U32)
    hp_ref[0] = (hi & jnp.uint32(0xFFFF0000)) | (lo >> 16)

    logits = jnp.dot(h, wr_ref[...], preferred_element_type=F32, precision=lax.Precision.HIGHEST)
    lane = lax.broadcasted_iota(I32, logits.shape, 1).astype(F32)
    neg = jnp.float32(-jnp.inf)
    l1 = jnp.where(lane < N_EXPERTS, logits, neg)
    m1 = jnp.max(l1, axis=-1, keepdims=True)
    i1 = jnp.min(jnp.where(l1 == m1, lane, float(LANES)), axis=-1, keepdims=True)
    l2 = jnp.where(lane == i1, neg, l1)
    m2 = jnp.max(l2, axis=-1, keepdims=True)
    i2 = jnp.min(jnp.where(l2 == m2, lane, float(LANES)), axis=-1, keepdims=True)
    e = jnp.exp(m2 - m1)
    den = 1.0 + e
    gcol_ref[...] = jnp.where(lane == 0, 1.0 / den, jnp.where(lane == 1, e / den, 0.0))

    chosen = jnp.where((lane == i1) | (lane == i2), 1.0, 0.0)
    chosen_t = chosen.T
    rr = lax.broadcasted_iota(I32, (tm, tm), 0)
    cc = lax.broadcasted_iota(I32, (tm, tm), 1)
    before = jnp.where(rr < cc, 1.0, 0.0).astype(BF16)
    cum = jnp.dot(chosen_t.astype(BF16), before, preferred_element_type=F32) + run_ref[...]
    sel = jnp.where(lane == 0, i1, jnp.where(lane == 1, i2, 0.0)).T
    i1r = sel[0:1, :]
    i2r = sel[1:2, :]
    sub = lax.broadcasted_iota(I32, cum.shape, 0).astype(F32)
    p1r = jnp.sum(jnp.where(sub == i1r, cum, 0.0), axis=0, keepdims=True)
    p2r = jnp.sum(jnp.where(sub == i2r, cum, 0.0), axis=0, keepdims=True)
    idx_ref[0] = jnp.concatenate([i1r, i2r, p1r, p2r], axis=0).astype(I32)

    run_ref[...] += jnp.sum(chosen_t, axis=1, keepdims=True)
    cnt_ref[...] = jnp.broadcast_to(run_ref[...], cnt_ref.shape)


def moe_router(x, g, shift, scale, w_router, *, tm=512):
    B, S, D = x.shape
    E = w_router.shape[1]
    tm = _tile(S, tm)
    nS = S // tm
    wr = jnp.zeros((D, LANES), F32).at[:, :E].set(w_router.astype(F32))
    vec = pl.BlockSpec((1, 1, D), lambda b, i: (b, 0, 0))
    hp, idx, gcol, cnt = pl.pallas_call(
        _router_kernel,
        grid=(B, nS),
        in_specs=[
            pl.BlockSpec((1, tm, D), lambda b, i: (b, i, 0)),
            pl.BlockSpec((1, D), lambda b, i: (0, 0)),
            vec, vec,
            pl.BlockSpec((D, LANES), lambda b, i: (0, 0)),
        ],
        out_specs=[
            pl.BlockSpec((1, tm, D // 2), lambda b, i: (b, i, 0)),
            pl.BlockSpec((1, 4, tm), lambda b, i: (b * nS + i, 0, 0)),
            pl.BlockSpec((tm, LANES), lambda b, i: (b * nS + i, 0)),
            pl.BlockSpec((LANES, LANES), lambda b, i: (0, 0)),
        ],
        out_shape=[
            jax.ShapeDtypeStruct((B, S, D // 2), U32),
            jax.ShapeDtypeStruct((B * nS, 4, tm), I32),
            jax.ShapeDtypeStruct((B * S, LANES), F32),
            jax.ShapeDtypeStruct((LANES, LANES), F32),
        ],
        scratch_shapes=[pltpu.VMEM((LANES, 1), F32)],
        compiler_params=_params(("arbitrary", "arbitrary")),
        name="moe_router",
    )(x, g.reshape(1, D), shift, scale, wr)
    return hp.reshape(B * S, D // 2), idx, gcol, cnt


def _row_copy(src, dst, sem):
    return pltpu.make_async_copy(src, dst, sem)


ROW_DMA_UNROLL = 8
ROW_WAIT_UNROLL = 64


def _wait_rows(n, src_row, dst_row, sem):
    def wait(i, c):
        for _ in range(ROW_WAIT_UNROLL):
            _row_copy(src_row, dst_row, sem).wait()
        return c

    assert n % ROW_WAIT_UNROLL == 0
    lax.fori_loop(0, n // ROW_WAIT_UNROLL, wait, 0)


def _dispatch_kernel(dest_ref, hp_ref, xs_in_ref, xs_ref, sem):
    del xs_in_ref
    tm = hp_ref.shape[0]

    def start(t, c):
        for k in range(2):
            _row_copy(hp_ref.at[pl.ds(t, 1)], xs_ref.at[pl.ds(dest_ref[0, k, t], 1)], sem).start()
        return c

    lax.fori_loop(0, tm, start, 0, unroll=ROW_DMA_UNROLL)
    _wait_rows(2 * tm, hp_ref.at[pl.ds(0, 1)], xs_ref.at[pl.ds(0, 1)], sem)


def moe_dispatch(dest, hp, n_slots):
    N, d2 = hp.shape
    nT, _, tm = dest.shape
    xs0 = jnp.zeros((n_slots, d2), U32)
    return pl.pallas_call(
        _dispatch_kernel,
        grid=(nT,),
        in_specs=[
            pl.BlockSpec((1, 2, tm), lambda t: (t, 0, 0), memory_space=pltpu.SMEM),
            pl.BlockSpec((tm, d2), lambda t: (t, 0)),
            pl.BlockSpec(memory_space=pl.ANY),
        ],
        out_specs=pl.BlockSpec(memory_space=pl.ANY),
        out_shape=jax.ShapeDtypeStruct((n_slots, d2), U32),
        scratch_shapes=[pltpu.SemaphoreType.DMA(())],
        input_output_aliases={2: 0},
        compiler_params=_params(("arbitrary",)),
        name="moe_dispatch",
    )(dest, hp, xs0)


def _moe_ffn_kernel(be_ref, nv_ref, xs_ref, wg_ref, wu_ref, wd_ref, y_ref, xb_ref):
    b = pl.program_id(0)
    j = pl.program_id(1)
    d2 = xs_ref.shape[1]

    @pl.when((b >= nv_ref[0]) & (j == 0))
    def _():
        y_ref[...] = jnp.zeros_like(y_ref)

    @pl.when(b < nv_ref[0])
    def _():
        @pl.when(j == 0)
        def _():
            w = xs_ref[...]
            xb_ref[:, :d2] = pltpu.bitcast(w << 16, F32).astype(BF16)
            xb_ref[:, d2:] = pltpu.bitcast(w & jnp.uint32(0xFFFF0000), F32).astype(BF16)

            y_ref[...] = jnp.zeros_like(y_ref)

        x = xb_ref[...]
        a = (jax.nn.silu(jnp.dot(x, wg_ref[0], preferred_element_type=F32))
             * jnp.dot(x, wu_ref[0], preferred_element_type=F32)).astype(BF16)
        _accumulate_down(y_ref, a, wd_ref.at[0])


def moe_expert_ffn(block_e, n_valid, xs, wg, wu, wd, *, tm, tf=512):
    n_slots, d2 = xs.shape
    D = 2 * d2
    FF = wg.shape[2]
    tf = _tile(FF, tf)
    nj = FF // tf
    nb = n_slots // tm

    def row_map(b, j, be, nv):
        return (jnp.minimum(b, nv[0] - 1), 0)

    def col_of(b, j, nv):
        return jnp.where(b < nv[0], j, nj - 1)

    grid_spec = pltpu.PrefetchScalarGridSpec(
        num_scalar_prefetch=2,
        grid=(nb, nj),
        in_specs=[
            pl.BlockSpec((tm, d2), row_map),
            pl.BlockSpec((1, D, tf), lambda b, j, be, nv: (be[b], 0, col_of(b, j, nv))),
            pl.BlockSpec((1, D, tf), lambda b, j, be, nv: (be[b], 0, col_of(b, j, nv))),
            pl.BlockSpec((1, tf, D), lambda b, j, be, nv: (be[b], col_of(b, j, nv), 0)),
        ],
        out_specs=pl.BlockSpec((tm, D), lambda b, j, be, nv: (b, 0)),
        scratch_shapes=[pltpu.VMEM((tm, D), BF16)],
    )
    return pl.pallas_call(
        _moe_ffn_kernel,
        grid_spec=grid_spec,
        out_shape=jax.ShapeDtypeStruct((n_slots, D), F32),
        compiler_params=_params(("arbitrary", "arbitrary")),
        name="moe_expert_ffn",
    )(block_e, n_valid, xs, wg, wu, wd)


def _combine_kernel(dest_ref, x_ref, gcol_ref, gate_ref, fg_ref, y_ref, o_ref, ya_ref, yb_ref, sem):
    tm = x_ref.shape[1]
    bufs = (ya_ref, yb_ref)

    def start(t, c):
        for k in range(2):
            _row_copy(y_ref.at[pl.ds(dest_ref[0, k, t], 1)], bufs[k].at[pl.ds(t, 1)], sem).start()
        return c

    lax.fori_loop(0, tm, start, 0, unroll=ROW_DMA_UNROLL)
    _wait_rows(2 * tm, y_ref.at[pl.ds(0, 1)], ya_ref.at[pl.ds(0, 1)], sem)

    moe = gcol_ref[:, 0:1] * ya_ref[...] + gcol_ref[:, 1:2] * yb_ref[...]
    x = x_ref[0] + gate_ref[0] * moe
    y = x * lax.rsqrt(jnp.mean(x * x, axis=-1, keepdims=True) + NORM_EPS)
    o_ref[0] = y * fg_ref[...]


def moe_combine(dest, x, gcol, gate, final_g, y_buf):
    B, S, D = x.shape
    nT, _, tm = dest.shape
    nS = S // tm
    return pl.pallas_call(
        _combine_kernel,
        grid=(B, nS),
        in_specs=[
            pl.BlockSpec((1, 2, tm), lambda b, i: (b * nS + i, 0, 0), memory_space=pltpu.SMEM),
            pl.BlockSpec((1, tm, D), lambda b, i: (b, i, 0)),
            pl.BlockSpec((tm, LANES), lambda b, i: (b * nS + i, 0)),
            pl.BlockSpec((1, 1, D), lambda b, i: (b, 0, 0)),
            pl.BlockSpec((1, D), lambda b, i: (0, 0)),
            pl.BlockSpec(memory_space=pl.ANY),
        ],
        out_specs=pl.BlockSpec((1, tm, D), lambda b, i: (b, i, 0)),
        out_shape=jax.ShapeDtypeStruct((B, S, D), F32),
        scratch_shapes=[pltpu.VMEM((tm, D), F32), pltpu.VMEM((tm, D), F32), pltpu.SemaphoreType.DMA(())],
        compiler_params=_params(("arbitrary", "arbitrary")),
        name="moe_combine",
    )(dest, x, gcol, gate, final_g.reshape(1, D), y_buf)


def moe_layout(cnt, idx, n_tokens, blk):
    counts = cnt[:N_EXPERTS, 0].astype(I32)
    padded = (counts + blk - 1) // blk * blk
    pad_end = jnp.cumsum(padded)
    pad_start = (pad_end - padded).astype(I32)
    n_blocks = (2 * n_tokens) // blk + N_EXPERTS
    n_valid = (pad_end[-1] // blk).astype(I32)
    starts = jnp.minimum(jnp.arange(n_blocks, dtype=I32), n_valid - 1) * blk
    block_e = jnp.minimum(jnp.sum(pad_end[None, :] <= starts[:, None], axis=1), N_EXPERTS - 1).astype(I32)
    e, pos = idx[:, 0:2, :], idx[:, 2:4, :]
    group_start = jnp.sum(jnp.where(e[..., None] == jnp.arange(N_EXPERTS, dtype=I32), pad_start, 0), axis=-1)
    dest = (pos + group_start).astype(I32)
    return dest, block_e, n_valid.reshape(1), n_blocks * blk


def rope_tables(n_tokens):
    t = jnp.arange(n_tokens)
    row = (t // GRID_W).astype(F32)
    col = (t % GRID_W).astype(F32)
    half = HEAD_DIM // 2
    freqs = ROPE_THETA ** (-jnp.arange(0, half, 2, dtype=F32) / half)
    ang = jnp.concatenate([row[:, None] * freqs, col[:, None] * freqs], axis=-1)
    cos = jnp.repeat(jnp.cos(ang), 2, axis=-1)
    sin = jnp.repeat(jnp.sin(ang), 2, axis=-1) * jnp.tile(jnp.array([-1.0, 1.0], F32), half)
    return cos, sin


def kernel(x, c, ctx, c_ctx, ada_w, ada_b, norm1_g, norm2_g, attn_w_in, attn_w_out, na_rpb, gqa_q_norm_g,
           gqa_k_norm_g, ffn_w_gate, ffn_w_up, ffn_w_down, conv_w_in, conv_w, conv_w_out, moe_w_router,
           moe_w_gate, moe_w_up, moe_w_down, final_norm_g):
    B, S, D = x.shape
    assert ada_w.shape[0] == 2 and S % GRID_W == 0 and S // GRID_W >= WIN_H
    kh = WIN_H

    n_cond = -(-(B + 1) // SUBLANES) * SUBLANES
    cond = jnp.zeros((n_cond, D), F32).at[:B].set(c).at[B].set(c_ctx)
    mod = adaln_terms(cond, ada_w, ada_b)

    def terms(layer, rows):
        m = mod[layer, rows].reshape(-1, 6, D)
        return [m[:, k][:, None, :] for k in range(6)]

    sh1, sc1, g1, sh2, sc2, g2 = terms(0, slice(0, B))
    csh1, csc1 = [jnp.broadcast_to(t, (B, 1, D)) for t in terms(0, slice(B, B + 1))[:2]]

    w_in = attn_w_in[0].astype(BF16)
    proj = norm_mod_matmul(x, norm1_g[0], sh1, sc1, w_in, tn=1536)
    kvc = norm_mod_matmul(ctx, norm1_g[0], csh1, csc1, w_in, col_off=Q_COLS, tm=256)
    cos, sin = rope_tables(S)
    q_scale = HEAD_DIM ** -0.5 * LOG2_E
    qn = head_norm(proj, NA_COLS // GQA_Q_COLS, GQA_Q_HEADS, gqa_q_norm_g[0], cos, sin, rope=True,
                   out_scale=q_scale)
    kn = head_norm(proj, (Q_COLS + 2 * NA_COLS) // GQA_KV_COLS, GQA_KV_HEADS, gqa_k_norm_g[0], cos, sin, rope=True)
    L = ctx.shape[1]
    kcn = head_norm(kvc, 2 * NA_COLS // GQA_KV_COLS, GQA_KV_HEADS, gqa_k_norm_g[0], cos[:L], sin[:L], rope=False,
                    tm=L)
    out_a = neighborhood_attention(proj, kvc, na_bias_table(na_rpb[0], kh), kh)
    score_bound = (HEAD_DIM * q_scale * BF16_ROUNDING_MARGIN
                   * jnp.max(jnp.abs(gqa_q_norm_g[0])) * jnp.max(jnp.abs(gqa_k_norm_g[0])))
    E, _, FF = moe_w_gate[0].shape
    moe_f32 = [moe_w_gate[0].reshape(E * D, FF), moe_w_up[0].reshape(E * D, FF), moe_w_down[0].reshape(E * FF, D)]
    if not SideCast.fits(moe_f32, gqa_grid(B, S)):
        moe_bf16, moe_f32 = [a.astype(BF16) for a in moe_f32], []
    out_b, *converted = lax.cond(
        score_bound <= GQA_UNSHIFTED_SCORE_LIMIT,
        functools.partial(gqa_attention, subtract_max=False),
        functools.partial(gqa_attention, subtract_max=True),
        qn, kn, proj, kcn, kvc, *moe_f32)
    moe_wg, moe_wu, moe_wd = converted if moe_f32 else moe_bf16
    x = matmul_residual([out_a, out_b], attn_w_out[0].astype(BF16), x, g1)

    x = dense_ffn(x, norm2_g[0], sh2, sc2, g2, ffn_w_gate[0].astype(BF16), ffn_w_up[0].astype(BF16),
                  ffn_w_down[0].astype(BF16))

    sh1, sc1, g1, sh2, sc2, g2 = terms(1, slice(0, B))
    z = conv_mixer_inner(x, norm1_g[1], sh1, sc1, conv_w_in[0].astype(BF16), conv_w[0])
    x = matmul_residual([z], conv_w_out[0].astype(BF16), x, g1)

    moe_blk = 1024 if (2 * B * S) % 1024 == 0 else 2 * B * S // N_EXPERTS
    hp, idx, gcol, cnt = moe_router(x, norm2_g[1], sh2, sc2, moe_w_router[0])
    dest, block_e, n_valid, n_slots = moe_layout(cnt, idx, B * S, moe_blk)
    xs = moe_dispatch(dest, hp, n_slots)
    y_buf = moe_expert_ffn(block_e, n_valid, xs, moe_wg.reshape(E, D, FF), moe_wu.reshape(E, D, FF),
                           moe_wd.reshape(E, FF, D), tm=moe_blk)
    return moe_combine(dest, x, gcol, g2, final_norm_g, y_buf)
```
